```python
import math
import jax
import jax.numpy as jnp
from jax import lax
import numpy as np

D_MODEL = 2048
BATCH = 8
SEQ = 2048
DEPTH = 2

GRID_W = 64
CTX_LEN = 256
EPS = 1e-6
N_SUB = 3
N_MOD = 3 * N_SUB
D_FF = 5632
MIX_WIDTH = D_MODEL
S5_WIDTH = D_MODEL // 2
S5_GROUP = 16
S5_GROUPS = S5_WIDTH // S5_GROUP
S5_STATE = 64
NA_HEADS = 8
NA_HEAD_DIM = 128
NA_WIDTH = NA_HEADS * NA_HEAD_DIM
NA_KH_MAX = 8
NA_KW = 16
EVEN_IN = S5_WIDTH + 3 * NA_WIDTH
HY_WIDTH = D_MODEL // 2
HY_ORDER = 2
HY_SHORT = 3
HY_BANDS = 16
HY_EMB = 1 + 2 * HY_BANDS
HY_HIDDEN = 64
HY_N_MID = 2
HY_DECAY_TARGET = 1e-2
HY_FAST_PCT = 0.3
HY_SLOW_PCT = 1.5
HY_FILTER_STD = 0.007
SSD_INNER = D_MODEL // 2
SSD_HEAD_DIM = 64
SSD_HEADS = SSD_INNER // SSD_HEAD_DIM
SSD_GROUPS = 4
SSD_STATE = 128
SSD_CONV = 3
SSD_CHUNK = 128
SSD_BC = SSD_GROUPS * SSD_STATE
SSD_XBC = SSD_INNER + 2 * SSD_BC
ODD_IN = 3 * HY_WIDTH + SSD_INNER + SSD_XBC + 2 * SSD_HEADS
F32 = jnp.float32

kernel_name = 'hybrid_s5_natten_hyena_ssd_prefix_dit'


def _rmsnorm(x, g):
    xf = x.astype(F32)
    y = xf * lax.rsqrt(jnp.mean(xf * xf, axis=-1, keepdims=True) + EPS)
    return (y * g.astype(F32)).astype(x.dtype)


def _pre(h, g, m, i):
    return _rmsnorm(h, g) * (1 + m[:, 3 * i + 1]) + m[:, 3 * i]


def _swiglu(h, wg, wu, wd):
    return (jax.nn.silu(h @ wg) * (h @ wu)) @ wd


def _dwconv(u, w, b):
    k = w.shape[0]
    y = lax.conv_general_dilated(u, w[:, None, :].astype(u.dtype), (1,), [(k // 2, k // 2)],
                                 dimension_numbers=('NWC', 'WIO', 'NWC'),
                                 feature_group_count=u.shape[-1])
    return y + b.astype(u.dtype)


def _lin_combine(left, right):
    a_l, b_l = left
    a_r, b_r = right
    return a_l * a_r, a_r * b_l + b_r


def _s5_discretize(a_re, a_im, log_dt, b_re, b_im):
    lam = lax.complex(a_re.astype(F32), a_im.astype(F32))
    dt = jnp.exp(log_dt.astype(F32))[:, None]
    a_bar = jnp.exp(lam * dt)
    b_bar = ((a_bar - 1.0) / lam)[..., None] * lax.complex(b_re.astype(F32), b_im.astype(F32))
    return a_bar, b_bar


def _s5_scan(u, a_bar, b_bar, s0, reverse):
    if reverse:
        u = jnp.flip(u, 1)
    bu = jnp.einsum('gph,blgh->blgp', b_bar, u.astype(jnp.complex64))
    if s0 is not None:
        bu = bu.at[:, 0].add(a_bar * s0)
    a = jnp.broadcast_to(a_bar, (1,) + bu.shape[1:])
    _, s = lax.associative_scan(_lin_combine, (a, bu), axis=1)
    final = s[:, -1]
    if reverse:
        s = jnp.flip(s, 1)
    return s, final


def _s5_readout(c_mat, s):
    return jnp.einsum('ghp,blgp->blgh', c_mat, s).real


def _s5_mixer(uc, ul, a_re, a_im, log_dt, b_re, b_im, c_re, c_im, d, glu_w, glu_b, ctx_out):
    def grp(u):
        return u.astype(F32).reshape(u.shape[0], u.shape[1], S5_GROUPS, S5_GROUP)
    uc, ul = grp(uc), grp(ul)
    d = d.astype(F32)
    yl = d * ul
    yc = d * uc if ctx_out else None
    for k, rev in ((0, False), (1, True)):
        a_bar, b_bar = _s5_discretize(a_re[k], a_im[k], log_dt[k], b_re[k], b_im[k])
        c_mat = lax.complex(c_re[k].astype(F32), c_im[k].astype(F32))
        sc, fin = _s5_scan(uc, a_bar, b_bar, None, rev)
        sl, _ = _s5_scan(ul, a_bar, b_bar, fin, rev)
        yl = yl + _s5_readout(c_mat, sl)
        if ctx_out:
            yc = yc + _s5_readout(c_mat, sc)

    def glu(y):
        y = jax.nn.gelu(y.reshape(y.shape[0], y.shape[1], S5_WIDTH))
        return y * jax.nn.sigmoid(y @ glu_w.astype(F32) + glu_b.astype(F32))
    return (glu(yc) if ctx_out else None), glu(yl)


def _split_qkv(p):
    qkv = p[..., S5_WIDTH:].reshape(p.shape[0], p.shape[1], 3, NA_HEADS, NA_HEAD_DIM)
    return qkv[:, :, 0], qkv[:, :, 1], qkv[:, :, 2]


def _na_latent(q, k, v, kc, vc, rpb):
    bsz, length, heads, hd = q.shape
    rows = length // GRID_W
    kh = min(NA_KH_MAX, rows)
    scale = hd ** -0.5
    qg = q.reshape(bsz, rows, GRID_W, heads, hd)
    kg = k.reshape(bsz, rows, GRID_W, heads, hd)
    vg = v.reshape(bsz, rows, GRID_W, heads, hd)
    col = jnp.arange(GRID_W)
    cs = jnp.clip(col - NA_KW // 2, 0, GRID_W - NA_KW)
    col_mask = (col[None, :] >= cs[:, None]) & (col[None, :] < cs[:, None] + NA_KW)
    dc_idx = jnp.clip(col[None, :] - col[:, None] + NA_KW - 1, 0, 2 * NA_KW - 2)
    rpb_c = rpb.astype(F32)[:, :, dc_idx]
    n_loc = kh * GRID_W

    def row_block(r):
        rs = jnp.clip(r - kh // 2, 0, rows - kh)
        qr = lax.dynamic_index_in_dim(qg, r, axis=1, keepdims=False)
        kr = lax.dynamic_slice_in_dim(kg, rs, kh, axis=1)
        vr = lax.dynamic_slice_in_dim(vg, rs, kh, axis=1)
        dr_idx = rs + jnp.arange(kh) - r + NA_KH_MAX - 1
        bias = jnp.transpose(rpb_c[:, dr_idx], (0, 2, 1, 3))
        s_loc = jnp.einsum('bqhd,bjkhd->bhqjk', qr, kr).astype(F32) * scale + bias[None]
        s_loc = jnp.where(col_mask[None, None, :, None, :], s_loc, -jnp.inf)
        s_ctx = jnp.einsum('bqhd,bchd->bhqc', qr, kc).astype(F32) * scale
        p = jax.nn.softmax(jnp.concatenate([s_loc.reshape(bsz, heads, GRID_W, n_loc), s_ctx], -1), axis=-1)
        p_loc = p[..., :n_loc].reshape(bsz, heads, GRID_W, kh, GRID_W).astype(v.dtype)
        p_ctx = p[..., n_loc:].astype(v.dtype)
        return (jnp.einsum('bhqjk,bjkhd->bqhd', p_loc, vr)
                + jnp.einsum('bhqc,bchd->bqhd', p_ctx, vc))

    out = lax.map(row_block, jnp.arange(rows))
    return jnp.transpose(out, (1, 0, 2, 3, 4)).reshape(bsz, length, heads * hd)


def _attend_dense(q, k, v):
    s = jnp.einsum('bqhd,bkhd->bhqk', q, k).astype(F32) * q.shape[-1] ** -0.5
    p = jax.nn.softmax(s, axis=-1).astype(v.dtype)
    return jnp.einsum('bhqk,bkhd->bqhd', p, v).reshape(q.shape[0], q.shape[1], -1)


def _even_mixer(hc, hl, w_in, w_out, s5_params, rpb, ctx_out):
    pc = hc @ w_in
    pl = hl @ w_in
    s5_c, s5_l = _s5_mixer(pc[..., :S5_WIDTH], pl[..., :S5_WIDTH], *s5_params, ctx_out)
    qc, kc, vc = _split_qkv(pc)
    ql, kl, vl = _split_qkv(pl)
    yl = jnp.concatenate([s5_l, _na_latent(ql, kl, vl, kc, vc, rpb)], axis=-1) @ w_out
    if not ctx_out:
        return None, yl
    yc = jnp.concatenate([s5_c, _attend_dense(qc, kc, vc)], axis=-1) @ w_out
    return yc, yl


def _hyena_filters(length, w_in, b_in, w_mid, b_mid, w_out, freq):
    t = jnp.linspace(0.0, 1.0, length, dtype=F32)[:, None]
    w = 2.0 * math.pi * jnp.arange(length, dtype=F32)[:, None] / length
    f = jnp.linspace(1e-4, HY_BANDS - 1, HY_BANDS, dtype=F32)[None, :]
    z = jnp.concatenate([t, jnp.cos(f * w), -jnp.sin(f * w)], axis=-1)
    freq = freq.astype(F32)
    h = jnp.sin(freq * (z @ w_in.astype(F32) + b_in.astype(F32)))
    for i in range(HY_N_MID):
        h = jnp.sin(freq * (h @ w_mid[i].astype(F32) + b_mid[i].astype(F32)))
    h = (h @ w_out.astype(F32)).reshape(length, HY_ORDER, 2, HY_WIDTH)
    max_decay = math.log(HY_DECAY_TARGET) / HY_FAST_PCT
    min_decay = math.log(HY_DECAY_TARGET) / HY_SLOW_PCT
    deltas = jnp.abs(jnp.linspace(min_decay, max_decay, HY_WIDTH, dtype=F32))
    decay = jnp.exp(-t * deltas[None, :])
    return h * decay[:, None, None, :]


def _bidir_fftconv(u, h_fwd, h_bwd, bias):
    length, ch = h_fwd.shape
    n = 2 * length
    k = jnp.concatenate([h_fwd, jnp.zeros((1, ch), F32), jnp.flip(h_bwd[1:], 0)], axis=0)
    uf = jnp.fft.rfft(u.astype(F32), n=n, axis=1)
    kf = jnp.fft.rfft(k, n=n, axis=0)
    y = jnp.fft.irfft(uf * kf[None], n=n, axis=1)[:, :length]
    return y + u.astype(F32) * bias.astype(F32)


def _hyena(p, short_w, short_b, w_in, b_in, w_mid, b_mid, w_out, freq, fbias):
    x1, x2, v = jnp.split(_dwconv(p, short_w, short_b).astype(F32), 3, axis=-1)
    h = _hyena_filters(p.shape[1], w_in, b_in, w_mid, b_mid, w_out, freq)
    z = x1 * _bidir_fftconv(v, h[:, 0, 0], h[:, 0, 1], fbias[0])
    return x2 * _bidir_fftconv(z, h[:, 1, 0], h[:, 1, 1], fbias[1])


def _ssd_chunked(x, dt, a, b, c, s0, with_output):
    bsz, length = x.shape[:2]
    q = min(SSD_CHUNK, length)
    nc = length // q
    e = SSD_HEADS // SSD_GROUPS
    xdt = (x * dt[..., None]).reshape(bsz, nc, q, SSD_GROUPS, e, SSD_HEAD_DIM)
    cum = jnp.cumsum((dt * a).reshape(bsz, nc, q, SSD_GROUPS, e), axis=2)
    bq = b.reshape(bsz, nc, q, SSD_GROUPS, SSD_STATE)
    xdt_w = xdt * jnp.exp(cum[:, :, -1:] - cum)[..., None]
    states = jnp.einsum('bcsgn,bcsgep->bcgepn', bq, xdt_w)
    chunk_decay = jnp.exp(cum[:, :, -1])
    if s0 is None:
        s0 = jnp.zeros((bsz, SSD_GROUPS, e, SSD_HEAD_DIM, SSD_STATE), F32)

    def step(s, inp):
        st, dec = inp
        return s * dec[..., None, None] + st, s
    s_final, s_prev = lax.scan(step, s0, (jnp.moveaxis(states, 1, 0), jnp.moveaxis(chunk_decay, 1, 0)))
    if not with_output:
        return None, s_final
    cq = c.reshape(bsz, nc, q, SSD_GROUPS, SSD_STATE)
    seg = cum[:, :, :, None] - cum[:, :, None, :]
    tri = jnp.tril(jnp.ones((q, q), bool))[None, None, :, :, None, None]
    lmat = jnp.exp(jnp.where(tri, seg, -jnp.inf))
    cb = jnp.einsum('bclgn,bcsgn->bclsg', cq, bq)
    y_diag = jnp.einsum('bclsge,bcsgep->bclgep', cb[..., None] * lmat, xdt)
    y_off = jnp.einsum('bclgn,cbgepn->bclgep', cq, s_prev) * jnp.exp(cum)[..., None]
    return (y_diag + y_off).reshape(bsz, length, SSD_HEADS, SSD_HEAD_DIM), s_final


def _ssd_bidir(xbc, dt_raw, dt_bias, a_log, s0, with_output):
    bsz, length = xbc.shape[:2]
    xbc = xbc.astype(F32)
    xs = xbc[..., :SSD_INNER].reshape(bsz, length, SSD_HEADS, SSD_HEAD_DIM)
    bm = xbc[..., SSD_INNER:SSD_INNER + SSD_BC].reshape(bsz, length, SSD_GROUPS, SSD_STATE)
    cm = xbc[..., SSD_INNER + SSD_BC:].reshape(bsz, length, SSD_GROUPS, SSD_STATE)
    ys, finals = [], []
    for k in range(2):
        dt = jax.nn.softplus(dt_raw[..., k * SSD_HEADS:(k + 1) * SSD_HEADS].astype(F32) + dt_bias[k].astype(F32))
        a = -jnp.exp(a_log[k].astype(F32))
        seq = (xs, dt, bm, cm)
        if k == 1:
            seq = tuple(jnp.flip(s, 1) for s in seq)
        y, fin = _ssd_chunked(seq[0], seq[1], a, seq[2], seq[3], None if s0 is None else s0[k], with_output)
        if with_output:
            ys.append(jnp.flip(y, 1) if k == 1 else y)
        finals.append(fin)
    return ys, xs, finals


def _odd_mixer(hc, hl, w_in, w_out, hy_params, conv_w, conv_b, dt_bias, a_log, d_skip, norm_g, ctx_out):
    o_z = 3 * HY_WIDTH
    o_xbc = o_z + SSD_INNER
    o_dt = o_xbc + SSD_XBC
    pc = hc @ w_in
    pl = hl @ w_in
    xbc_c = jax.nn.silu(_dwconv(pc[..., o_xbc:o_dt], conv_w, conv_b))
    xbc_l = jax.nn.silu(_dwconv(pl[..., o_xbc:o_dt], conv_w, conv_b))
    ys_c, xs_c, fin_c = _ssd_bidir(xbc_c, pc[..., o_dt:], dt_bias, a_log, None, ctx_out)
    ys_l, xs_l, _ = _ssd_bidir(xbc_l, pl[..., o_dt:], dt_bias, a_log, fin_c, True)

    def ssd_out(p, ys, xs):
        y = ys[0] + ys[1] + d_skip.astype(F32)[:, None] * xs
        y = y.reshape(xs.shape[0], xs.shape[1], SSD_INNER) * jax.nn.silu(p[..., o_z:o_xbc].astype(F32))
        return _rmsnorm(y, norm_g)

    yl = jnp.concatenate([_hyena(pl[..., :o_z], *hy_params), ssd_out(pl, ys_l, xs_l)], axis=-1) @ w_out
    if not ctx_out:
        return None, yl
    yc = jnp.concatenate([_hyena(pc[..., :o_z], *hy_params), ssd_out(pc, ys_c, xs_c)], axis=-1) @ w_out
    return yc, yl


def setup_inputs(seed: int = 0) -> dict:
    key = jax.random.key(seed)
    keys = jax.random.split(key, 48)
    it = iter(range(48))

    def nrm(shape, std):
        return std * jax.random.normal(keys[next(it)], shape, F32)

    def unif(shape, lo, hi):
        return jax.random.uniform(keys[next(it)], shape, F32, lo, hi)

    ne, no = (DEPTH + 1) // 2, DEPTH // 2
    dt0 = jnp.exp(unif((no, 2, SSD_HEADS), math.log(1e-3), math.log(1e-1)))
    return {
        'x': nrm((BATCH, SEQ, D_MODEL), 1.0),
        'c': nrm((BATCH, D_MODEL), 1.0),
        'ctx': nrm((BATCH, CTX_LEN, D_MODEL), 1.0),
        'c_ctx': nrm((D_MODEL,), 1.0),
        'mod_w': nrm((DEPTH, D_MODEL, N_MOD * D_MODEL), 0.5 * D_MODEL ** -0.5),
        'mod_b': nrm((DEPTH, N_MOD * D_MODEL), 0.02),
        'norm_g': 1.0 + nrm((DEPTH, N_SUB, D_MODEL), 0.02),
        'ffn_wg': nrm((DEPTH, 2, D_MODEL, D_FF), D_MODEL ** -0.5),
        'ffn_wu': nrm((DEPTH, 2, D_MODEL, D_FF), D_MODEL ** -0.5),
        'ffn_wd': nrm((DEPTH, 2, D_FF, D_MODEL), D_FF ** -0.5),
        'final_g': 1.0 + nrm((D_MODEL,), 0.02),
        'ev_w_in': nrm((ne, D_MODEL, EVEN_IN), D_MODEL ** -0.5),
        'ev_w_out': nrm((ne, MIX_WIDTH, D_MODEL), MIX_WIDTH ** -0.5),
        's5_a_re': -0.5 + nrm((ne, 2, S5_GROUPS, S5_STATE), 0.01),
        's5_a_im': math.pi * jnp.arange(S5_STATE, dtype=F32) + nrm((ne, 2, S5_GROUPS, S5_STATE), 0.01),
        's5_log_dt': unif((ne, 2, S5_GROUPS), math.log(1e-3), math.log(1e-1)),
        's5_b_re': nrm((ne, 2, S5_GROUPS, S5_STATE, S5_GROUP), (2 * S5_GROUP) ** -0.5),
        's5_b_im': nrm((ne, 2, S5_GROUPS, S5_STATE, S5_GROUP), (2 * S5_GROUP) ** -0.5),
        's5_c_re': nrm((ne, 2, S5_GROUPS, S5_GROUP, S5_STATE), S5_STATE ** -0.5),
        's5_c_im': nrm((ne, 2, S5_GROUPS, S5_GROUP, S5_STATE), S5_STATE ** -0.5),
        's5_d': nrm((ne, S5_GROUPS, S5_GROUP), 1.0),
        's5_glu_w': nrm((ne, S5_WIDTH, S5_WIDTH), S5_WIDTH ** -0.5),
        's5_glu_b': nrm((ne, S5_WIDTH), 0.02),
        'na_rpb': nrm((ne, NA_HEADS, 2 * NA_KH_MAX - 1, 2 * NA_KW - 1), 0.1),
        'od_w_in': nrm((no, D_MODEL, ODD_IN), D_MODEL ** -0.5),
        'od_w_out': nrm((no, MIX_WIDTH, D_MODEL), MIX_WIDTH ** -0.5),
        'hy_short_w': nrm((no, HY_SHORT, 3 * HY_WIDTH), HY_SHORT ** -0.5),
        'hy_short_b': nrm((no, 3 * HY_WIDTH), 0.02),
        'hy_w_in': nrm((no, HY_EMB, HY_HIDDEN), HY_EMB ** -0.5),
        'hy_b_in': nrm((no, HY_HIDDEN), 0.1),
        'hy_w_mid': nrm((no, HY_N_MID, HY_HIDDEN, HY_HIDDEN), HY_HIDDEN ** -0.5),
        'hy_b_mid': nrm((no, HY_N_MID, HY_HIDDEN), 0.1),
        'hy_w_out': nrm((no, HY_HIDDEN, HY_ORDER * 2 * HY_WIDTH), HY_FILTER_STD),
        'hy_freq': 1.0 + nrm((no, HY_HIDDEN), 0.02),
        'hy_fbias': nrm((no, HY_ORDER, HY_WIDTH), 0.5),
        'ssd_conv_w': nrm((no, SSD_CONV, SSD_XBC), SSD_CONV ** -0.5),
        'ssd_conv_b': nrm((no, SSD_XBC), 0.02),
        'ssd_dt_bias': dt0 + jnp.log(-jnp.expm1(-dt0)),
        'ssd_a_log': jnp.log(unif((no, 2, SSD_HEADS), 1.0, 16.0)),
        'ssd_d': 1.0 + nrm((no, SSD_HEADS), 0.1),
        'ssd_norm_g': 1.0 + nrm((no, SSD_INNER), 0.02),
    }


def reference(x, c, ctx, c_ctx, mod_w, mod_b, norm_g, ffn_wg, ffn_wu, ffn_wd, final_g,
              ev_w_in, ev_w_out, s5_a_re, s5_a_im, s5_log_dt, s5_b_re, s5_b_im, s5_c_re, s5_c_im,
              s5_d, s5_glu_w, s5_glu_b, na_rpb,
              od_w_in, od_w_out, hy_short_w, hy_short_b, hy_w_in, hy_b_in, hy_w_mid, hy_b_mid,
              hy_w_out, hy_freq, hy_fbias,
              ssd_conv_w, ssd_conv_b, ssd_dt_bias, ssd_a_log, ssd_d, ssd_norm_g):
    bsz = c.shape[0]
    xl, xc = x, ctx
    for layer in range(DEPTH):
        last = layer == DEPTH - 1
        j = layer // 2
        ml = (jax.nn.silu(c) @ mod_w[layer] + mod_b[layer]).reshape(bsz, N_MOD, 1, D_MODEL)
        mc = (jax.nn.silu(c_ctx)[None] @ mod_w[layer] + mod_b[layer]).reshape(1, N_MOD, 1, D_MODEL)
        ffn_a = (ffn_wg[layer, 0], ffn_wu[layer, 0], ffn_wd[layer, 0])
        ffn_b = (ffn_wg[layer, 1], ffn_wu[layer, 1], ffn_wd[layer, 1])
        xl = xl + 0.5 * ml[:, 2] * _swiglu(_pre(xl, norm_g[layer, 0], ml, 0), *ffn_a)
        xc = xc + 0.5 * mc[:, 2] * _swiglu(_pre(xc, norm_g[layer, 0], mc, 0), *ffn_a)
        hl = _pre(xl, norm_g[layer, 1], ml, 1)
        hc = _pre(xc, norm_g[layer, 1], mc, 1)
        if layer % 2 == 0:
            s5_params = (s5_a_re[j], s5_a_im[j], s5_log_dt[j], s5_b_re[j], s5_b_im[j],
                         s5_c_re[j], s5_c_im[j], s5_d[j], s5_glu_w[j], s5_glu_b[j])
            yc, yl = _even_mixer(hc, hl, ev_w_in[j], ev_w_out[j], s5_params, na_rpb[j], not last)
        else:
            hy_params = (hy_short_w[j], hy_short_b[j], hy_w_in[j], hy_b_in[j], hy_w_mid[j],
                         hy_b_mid[j], hy_w_out[j], hy_freq[j], hy_fbias[j])
            yc, yl = _odd_mixer(hc, hl, od_w_in[j], od_w_out[j], hy_params, ssd_conv_w[j], ssd_conv_b[j],
                                ssd_dt_bias[j], ssd_a_log[j], ssd_d[j], ssd_norm_g[j], not last)
        xl = xl + ml[:, 5] * yl
        xl = xl + 0.5 * ml[:, 8] * _swiglu(_pre(xl, norm_g[layer, 2], ml, 2), *ffn_b)
        if not last:
            xc = xc + mc[:, 5] * yc
            xc = xc + 0.5 * mc[:, 8] * _swiglu(_pre(xc, norm_g[layer, 2], mc, 2), *ffn_b)
    return _rmsnorm(xl, final_g)
```

```python
import functools
import math

import numpy as np
import jax
import jax.numpy as jnp
from jax import lax
from jax.experimental import pallas as pl
from jax.experimental.pallas import tpu as pltpu

F32 = jnp.float32
BF16 = jnp.bfloat16

EPS = 1e-6
GRID_W = 64
N_MOD = 9
S5_GROUP = 16
S5_STATE = 64
NA_HEAD_DIM = 128
NA_KH = 8
NA_KW = 16
HY_BANDS = 16
HY_DECAY_TARGET = 1e-2
HY_FAST_PCT = 0.3
HY_SLOW_PCT = 1.5
SSD_HEAD_DIM = 64
SSD_GROUPS = 4
SSD_STATE = 128
SSD_CHUNK = 128

V7X_LANES = 128
V7X_SUBLANES = 8
V7X_VMEM_BYTES = 64 * 1024 * 1024
VMEM_LIMIT = V7X_VMEM_BYTES - 12 * 1024 * 1024

TM = 512
TF = 512
TN = 1024
S5_TC = 32
S5_W = 512
NA_GQ = 4
HY_TC = 512
HY_TK = 256
NEG = -1e30


def _cp(sem, vmem=VMEM_LIMIT):
    return pltpu.CompilerParams(dimension_semantics=sem, vmem_limit_bytes=vmem)


def _sigmoid(x):
    return 1.0 / (1.0 + jnp.exp(-x))


def _silu(x):
    return x * _sigmoid(x)


def _dot(a, b):
    return jnp.dot(a, b, preferred_element_type=F32)


def _dot_nt(a, b):
    return lax.dot_general(a, b, (((1,), (1,)), ((), ())), preferred_element_type=F32)


def _split2(x):
    hi = x.astype(BF16)
    lo = (x - hi.astype(F32)).astype(BF16)
    return hi, lo


def _dot3(a_hi, a_lo, b_hi, b_lo):
    return _dot(a_hi, b_hi) + (_dot(a_hi, b_lo) + _dot(a_lo, b_hi))


def _prenorm(x, g, shift, scale):
    y = x * lax.rsqrt(jnp.mean(x * x, axis=-1, keepdims=True) + EPS)
    return y * g * (1.0 + scale) + shift


def _mod_kernel(c_ref, w_ref, b_ref, o_ref):
    cs = _silu(c_ref[...]).astype(BF16)
    o_ref[0] = _dot(cs, w_ref[0].astype(BF16)) + b_ref[0]


def _modulation(cond, mod_w, mod_b):
    depth, d, n = mod_w.shape
    rows = cond.shape[0]
    return pl.pallas_call(
        _mod_kernel,
        grid=(depth, n // TN),
        in_specs=[pl.BlockSpec((rows, d), lambda l, j: (0, 0)),
                  pl.BlockSpec((1, d, TN), lambda l, j: (l, 0, j)),
                  pl.BlockSpec((1, 1, TN), lambda l, j: (l, 0, j))],
        out_specs=pl.BlockSpec((1, rows, TN), lambda l, j: (l, 0, j)),
        out_shape=jax.ShapeDtypeStruct((depth, rows, n), F32),
        compiler_params=_cp(("arbitrary", "arbitrary")),
    )(cond, mod_w, mod_b.reshape(depth, 1, n))


def _ffn_kernel(x_ref, mod_ref, g_ref, wg_ref, wu_ref, wd_ref, fg_ref, o_ref, xn_ref, acc_ref, *, sub, final):
    f = pl.program_id(1)

    @pl.when(f == 0)
    def _():
        xn = _prenorm(x_ref[...], g_ref[...], mod_ref[0, 3 * sub:3 * sub + 1, :], mod_ref[0, 3 * sub + 1:3 * sub + 2, :])
        xn_ref[...] = xn.astype(BF16)
        acc_ref[...] = jnp.zeros_like(acc_ref)

    xn = xn_ref[...]
    h = (_silu(_dot(xn, wg_ref[...])) * _dot(xn, wu_ref[...])).astype(BF16)
    acc_ref[...] += _dot(h, wd_ref[...])

    @pl.when(f == pl.num_programs(1) - 1)
    def _():
        y = x_ref[...] + 0.5 * mod_ref[0, 3 * sub + 2:3 * sub + 3, :] * acc_ref[...]
        if final:
            y = y * lax.rsqrt(jnp.mean(y * y, axis=-1, keepdims=True) + EPS) * fg_ref[...]
        o_ref[...] = y


def _ffn(x, mods, mod_of_tile, norm_g, wg, wu, wd, final_g, sub, final, tm):
    n_tok, d = x.shape
    ff = wg.shape[1]
    return pl.pallas_call(
        functools.partial(_ffn_kernel, sub=sub, final=final),
        grid=(n_tok // tm, ff // TF),
        in_specs=[pl.BlockSpec((tm, d), lambda i, f: (i, 0)),
                  pl.BlockSpec((1, N_MOD, d), lambda i, f: (mod_of_tile(i), 0, 0)),
                  pl.BlockSpec((1, d), lambda i, f: (0, 0)),
                  pl.BlockSpec((d, TF), lambda i, f: (0, f)),
                  pl.BlockSpec((d, TF), lambda i, f: (0, f)),
                  pl.BlockSpec((TF, d), lambda i, f: (f, 0)),
                  pl.BlockSpec((1, d), lambda i, f: (0, 0))],
        out_specs=pl.BlockSpec((tm, d), lambda i, f: (i, 0)),
        out_shape=jax.ShapeDtypeStruct((n_tok, d), F32),
        scratch_shapes=[pltpu.VMEM((tm, d), BF16), pltpu.VMEM((tm, d), F32)],
        compiler_params=_cp(("parallel", "arbitrary")),
    )(x, mods, norm_g.reshape(1, d), wg, wu, wd, final_g.reshape(1, d))


def _inproj_even_kernel(x_ref, mod_ref, g_ref, w_ref, u_ref, qkv_ref, xn_ref):
    n = pl.program_id(1)

    @pl.when(n == 0)
    def _():
        xn = _prenorm(x_ref[...], g_ref[...], mod_ref[0, 3:4, :], mod_ref[0, 4:5, :])
        xn_ref[...] = xn.astype(BF16)

    res = _dot(xn_ref[...], w_ref[...])

    @pl.when(n == 0)
    def _():
        u_ref[...] = res

    @pl.when(n > 0)
    def _():
        qkv_ref[...] = res.astype(BF16)


def _inproj_even(x, mods, mod_of_tile, norm_g, w, seq, nb, tm):
    n_tok, d = x.shape
    n_in = w.shape[1]
    s5w = TN
    tps = seq // tm
    return pl.pallas_call(
        _inproj_even_kernel,
        grid=(n_tok // tm, n_in // TN),
        in_specs=[pl.BlockSpec((tm, d), lambda i, n: (i, 0)),
                  pl.BlockSpec((1, N_MOD, d), lambda i, n: (mod_of_tile(i), 0, 0)),
                  pl.BlockSpec((1, d), lambda i, n: (0, 0)),
                  pl.BlockSpec((d, TN), lambda i, n: (0, n))],
        out_specs=[pl.BlockSpec((tm, s5w), lambda i, n: (i % tps, i // tps)),
                   pl.BlockSpec((tm, TN), lambda i, n: (i, jnp.maximum(n - 1, 0)))],
        out_shape=[jax.ShapeDtypeStruct((seq, nb * s5w), F32),
                   jax.ShapeDtypeStruct((n_tok, n_in - s5w), BF16)],
        scratch_shapes=[pltpu.VMEM((tm, d), BF16)],
        compiler_params=_cp(("parallel", "arbitrary")),
    )(x, mods, norm_g.reshape(1, d), w)


def _inproj_odd_kernel(x_ref, mod_ref, g_ref, w_ref, wdt_ref, p_ref, dt_ref, xn_ref):
    n = pl.program_id(1)

    @pl.when(n == 0)
    def _():
        xn = _prenorm(x_ref[...], g_ref[...], mod_ref[0, 3:4, :], mod_ref[0, 4:5, :])
        xn_ref[...] = xn.astype(BF16)
        dt_ref[...] = _dot(xn_ref[...], wdt_ref[...])

    p_ref[...] = _dot(xn_ref[...], w_ref[...])


def _inproj_odd(x, mods, mod_of_tile, norm_g, w, wdt, tm):
    n_tok, d = x.shape
    n_in = w.shape[1]
    n_dt = wdt.shape[1]
    return pl.pallas_call(
        _inproj_odd_kernel,
        grid=(n_tok // tm, n_in // TN),
        in_specs=[pl.BlockSpec((tm, d), lambda i, n: (i, 0)),
                  pl.BlockSpec((1, N_MOD, d), lambda i, n: (mod_of_tile(i), 0, 0)),
                  pl.BlockSpec((1, d), lambda i, n: (0, 0)),
                  pl.BlockSpec((d, TN), lambda i, n: (0, n)),
                  pl.BlockSpec((d, n_dt), lambda i, n: (0, 0))],
        out_specs=[pl.BlockSpec((tm, TN), lambda i, n: (i, n)),
                   pl.BlockSpec((tm, n_dt), lambda i, n: (i, 0))],
        out_shape=[jax.ShapeDtypeStruct((n_tok, n_in), F32),
                   jax.ShapeDtypeStruct((n_tok, n_dt), F32)],
        scratch_shapes=[pltpu.VMEM((tm, d), BF16)],
        compiler_params=_cp(("parallel", "arbitrary")),
    )(x, mods, norm_g.reshape(1, d), w, wdt)


def _outproj_even_kernel(a_ref, b_ref, x_ref, mod_ref, wa_ref, wb_ref, o_ref):
    y = _dot(a_ref[...], wa_ref[...]) + _dot(b_ref[...], wb_ref[...])
    o_ref[...] = x_ref[...] + mod_ref[0, 5:6, :] * y


def _outproj_even(a, b, x, mods, mod_of_tile, w, tm):
    n_tok, d = x.shape
    half = a.shape[1]
    kb = half // TN
    return pl.pallas_call(
        _outproj_even_kernel,
        grid=(n_tok // tm, d // TN),
        in_specs=[pl.BlockSpec((tm, half), lambda i, n: (i, 0)),
                  pl.BlockSpec((tm, half), lambda i, n: (i, 0)),
                  pl.BlockSpec((tm, TN), lambda i, n: (i, n)),
                  pl.BlockSpec((1, N_MOD, TN), lambda i, n: (mod_of_tile(i), 0, n)),
                  pl.BlockSpec((half, TN), lambda i, n: (0, n)),
                  pl.BlockSpec((half, TN), lambda i, n: (kb, n))],
        out_specs=pl.BlockSpec((tm, TN), lambda i, n: (i, n)),
        out_shape=jax.ShapeDtypeStruct((n_tok, d), F32),
        compiler_params=_cp(("parallel", "arbitrary")),
    )(a, b, x, mods, w, w)


def _outproj_odd_kernel(hy_ref, ys_ref, z_ref, ng_ref, x_ref, mod_ref, wa_ref, wb_ref, o_ref, b_ref):
    n = pl.program_id(1)

    @pl.when(n == 0)
    def _():
        y = ys_ref[...] * _silu(z_ref[...])
        y = y * lax.rsqrt(jnp.mean(y * y, axis=-1, keepdims=True) + EPS) * ng_ref[...]
        b_ref[...] = y.astype(BF16)

    y = _dot(hy_ref[...].astype(BF16), wa_ref[...]) + _dot(b_ref[...], wb_ref[...])
    o_ref[...] = x_ref[...] + mod_ref[0, 5:6, :] * y


def _outproj_odd(hy, ys, p, z_block, norm_g, x, mods, mod_of_tile, w, tm):
    n_tok, d = x.shape
    half = hy.shape[1]
    kb = half // TN
    return pl.pallas_call(
        _outproj_odd_kernel,
        grid=(n_tok // tm, d // TN),
        in_specs=[pl.BlockSpec((tm, half), lambda i, n: (i, 0)),
                  pl.BlockSpec((tm, half), lambda i, n: (i, 0)),
                  pl.BlockSpec((tm, half), lambda i, n: (i, z_block)),
                  pl.BlockSpec((1, half), lambda i, n: (0, 0)),
                  pl.BlockSpec((tm, TN), lambda i, n: (i, n)),
                  pl.BlockSpec((1, N_MOD, TN), lambda i, n: (mod_of_tile(i), 0, n)),
                  pl.BlockSpec((half, TN), lambda i, n: (0, n)),
                  pl.BlockSpec((half, TN), lambda i, n: (kb, n))],
        out_specs=pl.BlockSpec((tm, TN), lambda i, n: (i, n)),
        out_shape=jax.ShapeDtypeStruct((n_tok, d), F32),
        scratch_shapes=[pltpu.VMEM((tm, half), BF16)],
        compiler_params=_cp(("parallel", "arbitrary")),
    )(hy, ys, p, norm_g.reshape(1, half), x, mods, w, w)


def _s5_kernel(uc_ref, ul_ref, a_ref, bre_ref, bim_ref, cre_ref, cim_ref, yc_ref, yl_ref,
               s_ref, carry_ref, *, nc_c, tc, nb, kt_n, half):
    d = pl.program_id(0)
    j = pl.program_id(1)
    kin = bre_ref.shape[2]
    kst = bre_ref.shape[3]

    @pl.when(j == 0)
    def _():
        carry_ref[...] = jnp.zeros_like(carry_ref)

    def run(u_ref, y_ref):
        u = u_ref[...].astype(BF16)
        for kt in range(kt_n):
            uk = u[:, kt * kin:(kt + 1) * kin]
            s_ref[:, kt * kst:(kt + 1) * kst] = _dot(uk, bre_ref[0, kt])
            s_ref[:, half + kt * kst:half + (kt + 1) * kst] = _dot(uk, bim_ref[0, kt])
        for c0 in range(0, half, S5_W):
            a_re = jnp.broadcast_to(a_ref[0, 0:1, c0:c0 + S5_W], (nb, S5_W))
            a_im = jnp.broadcast_to(a_ref[0, 1:2, c0:c0 + S5_W], (nb, S5_W))

            def step(i, st):
                s_re, s_im = st
                t = jnp.where(d == 0, i, tc - 1 - i)
                row = pl.multiple_of(t * nb, nb)
                n_re = a_re * s_re - a_im * s_im + s_ref[pl.ds(row, nb), c0:c0 + S5_W]
                n_im = a_re * s_im + a_im * s_re + s_ref[pl.ds(row, nb), half + c0:half + c0 + S5_W]
                s_ref[pl.ds(row, nb), c0:c0 + S5_W] = n_re
                s_ref[pl.ds(row, nb), half + c0:half + c0 + S5_W] = n_im
                return n_re, n_im

            s_re, s_im = lax.fori_loop(0, tc, step, (carry_ref[:, c0:c0 + S5_W], carry_ref[:, half + c0:half + c0 + S5_W]),
                                       unroll=4)
            carry_ref[:, c0:c0 + S5_W] = s_re
            carry_ref[:, half + c0:half + c0 + S5_W] = s_im
        for kt in range(kt_n):
            s_re = s_ref[:, kt * kst:(kt + 1) * kst].astype(BF16)
            s_im = s_ref[:, half + kt * kst:half + (kt + 1) * kst].astype(BF16)
            y_ref[0, :, kt * kin:(kt + 1) * kin] = _dot(s_re, cre_ref[0, kt]) + _dot(s_im, cim_ref[0, kt])

    @pl.when(j < nc_c)
    def _():
        run(uc_ref, yc_ref)

    @pl.when(j >= nc_c)
    def _():
        run(ul_ref, yl_ref)


def _s5_scan(uc, ul, a_bar, bre, bim, cre, cim, nb):
    rows_c, w = uc.shape
    rows_l = ul.shape[0]
    r = S5_TC * nb
    nc_c, nc_l = rows_c // r, rows_l // r
    kt_n = bre.shape[1]
    half = a_bar.shape[2]

    def c_idx(d, j):
        jc = jnp.minimum(j, nc_c - 1)
        return jnp.where(d == 0, jc, nc_c - 1 - jc)

    def l_idx(d, j):
        jl = jnp.maximum(j - nc_c, 0)
        return jnp.where(d == 0, jl, nc_l - 1 - jl)

    wspec = lambda shp: pl.BlockSpec((1,) + shp, lambda d, j: (d,) + (0,) * len(shp))
    return pl.pallas_call(
        functools.partial(_s5_kernel, nc_c=nc_c, tc=S5_TC, nb=nb, kt_n=kt_n, half=half),
        grid=(2, nc_c + nc_l),
        in_specs=[pl.BlockSpec((r, w), lambda d, j: (c_idx(d, j), 0)),
                  pl.BlockSpec((r, w), lambda d, j: (l_idx(d, j), 0)),
                  wspec(a_bar.shape[1:]), wspec(bre.shape[1:]), wspec(bim.shape[1:]),
                  wspec(cre.shape[1:]), wspec(cim.shape[1:])],
        out_specs=[pl.BlockSpec((1, r, w), lambda d, j: (d, c_idx(d, j), 0)),
                   pl.BlockSpec((1, r, w), lambda d, j: (d, l_idx(d, j), 0))],
        out_shape=[jax.ShapeDtypeStruct((2, rows_c, w), F32), jax.ShapeDtypeStruct((2, rows_l, w), F32)],
        scratch_shapes=[pltpu.VMEM((r, 2 * half), F32), pltpu.VMEM((nb, 2 * half), F32)],
        compiler_params=_cp(("arbitrary", "arbitrary")),
    )(uc, ul, a_bar, bre, bim, cre, cim)


def _s5_params(a_re, a_im, log_dt, b_re, b_im, c_re, c_im):
    lam = lax.complex(a_re.astype(F32), a_im.astype(F32))
    dt = jnp.exp(log_dt.astype(F32))[..., None]
    a_bar = jnp.exp(lam * dt)
    b_bar = ((a_bar - 1.0) / lam)[..., None] * lax.complex(b_re.astype(F32), b_im.astype(F32))
    groups, p_dim, h_dim = b_bar.shape[1:]
    gl = V7X_LANES * 2 // h_dim
    kt = groups // gl
    eye = jnp.eye(gl, dtype=F32)
    a_pack = jnp.stack([a_bar.real.reshape(2, groups * p_dim), a_bar.imag.reshape(2, groups * p_dim)], axis=1)

    def pack_b(x):
        x = x.reshape(2, kt, gl, p_dim, h_dim)
        return jnp.einsum('lm,dklph->dklhmp', eye, x).reshape(2, kt, gl * h_dim, gl * p_dim).astype(BF16)

    def pack_c(x):
        x = x.reshape(2, kt, gl, h_dim, p_dim)
        return jnp.einsum('lm,dklhp->dklpmh', eye, x).reshape(2, kt, gl * p_dim, gl * h_dim).astype(BF16)

    return (a_pack, pack_b(b_bar.real), pack_b(b_bar.imag),
            pack_c(c_re.astype(F32)), pack_c(-c_im.astype(F32)))


def _s5_glu_kernel(u_ref, y_ref, d_ref, w_ref, b_ref, o_ref):
    y = d_ref[...] * u_ref[...] + y_ref[0] + y_ref[1]
    g = 0.5 * y * (1.0 + jnp.tanh(math.sqrt(2.0 / math.pi) * (y + 0.044715 * (y * y * y))))
    o_ref[...] = (g * _sigmoid(_dot(g.astype(BF16), w_ref[...]) + b_ref[...])).astype(BF16)


def _s5_glu(u_tm, y_tm, d_skip, glu_w, glu_b, seq, nb, tm):
    w = glu_w.shape[0]
    tps = seq // tm
    return pl.pallas_call(
        _s5_glu_kernel,
        grid=(nb, tps),
        in_specs=[pl.BlockSpec((tm, w), lambda b, i: (i, b)),
                  pl.BlockSpec((2, tm, w), lambda b, i: (0, i, b)),
                  pl.BlockSpec((1, w), lambda b, i: (0, 0)),
                  pl.BlockSpec((w, w), lambda b, i: (0, 0)),
                  pl.BlockSpec((1, w), lambda b, i: (0, 0))],
        out_specs=pl.BlockSpec((tm, w), lambda b, i: (b * tps + i, 0)),
        out_shape=jax.ShapeDtypeStruct((nb * seq, w), BF16),
        compiler_params=_cp(("parallel", "arbitrary")),
    )(u_tm.reshape(seq, nb * w), y_tm.reshape(2, seq, nb * w), d_skip.reshape(1, w), glu_w, glu_b.reshape(1, w))


def _na_groups(rows):
    wr = NA_GQ + NA_KH - 1
    plan, sigs = [], {}
    for g in range(rows // NA_GQ):
        r0 = g * NA_GQ
        ws = min(max(r0 - NA_KH // 2, 0), rows - wr)
        sig = tuple(min(max(r - NA_KH // 2, 0), rows - NA_KH) - ws for r in range(r0, r0 + NA_GQ)) + (r0 - ws,)
        tid = sigs.setdefault(sig, len(sigs))
        plan.append((r0, ws, tid))
    reps = [next(p for p in plan if p[2] == t) for t in range(len(sigs))]
    return plan, reps, wr


def _na_bias(rpb, rows):
    plan, reps, wr = _na_groups(rows)
    tabs_dr, tabs_dc, tabs_ok = [], [], []
    col = np.arange(GRID_W)
    cs = np.clip(col - NA_KW // 2, 0, GRID_W - NA_KW)
    col_ok = (col[None, :] >= cs[:, None]) & (col[None, :] < cs[:, None] + NA_KW)
    dc = np.clip(col[None, :] - col[:, None] + NA_KW - 1, 0, 2 * NA_KW - 2)
    for r0, ws, _ in reps:
        r = r0 + np.arange(NA_GQ)
        rs = np.clip(r - NA_KH // 2, 0, rows - NA_KH)
        kr = ws + np.arange(wr)
        row_ok = (kr[None, :] >= rs[:, None]) & (kr[None, :] < rs[:, None] + NA_KH)
        dr = np.clip(kr[None, :] - r[:, None] + NA_KH - 1, 0, 2 * NA_KH - 2)
        ok = row_ok[:, None, :, None] & col_ok[None, :, None, :]
        shape = ok.shape
        tabs_ok.append(ok.reshape(NA_GQ * GRID_W, wr * GRID_W))
        tabs_dr.append(np.broadcast_to(dr[:, None, :, None], shape).reshape(NA_GQ * GRID_W, wr * GRID_W))
        tabs_dc.append(np.broadcast_to(dc[None, :, None, :], shape).reshape(NA_GQ * GRID_W, wr * GRID_W))
    dr_i, dc_i, ok_m = np.stack(tabs_dr), np.stack(tabs_dc), np.stack(tabs_ok)
    bias = rpb.astype(F32)[:, dr_i, dc_i]
    bias = jnp.where(ok_m[None], bias, NEG)
    return jnp.transpose(bias, (1, 0, 2, 3))


def _na_kernel(q_ref, k_ref, v_ref, qc_ref, kc_ref, vc_ref, bias_ref, o_ref, oc_ref, *, plan, wr, scale):
    kc = kc_ref[0]
    vc = vc_ref[0]
    nq = NA_GQ * GRID_W
    nk = wr * GRID_W
    for r0, ws, tid in plan:
        q = q_ref[0, r0 * GRID_W:r0 * GRID_W + nq, :]
        kw = k_ref[0, ws * GRID_W:ws * GRID_W + nk, :]
        vw = v_ref[0, ws * GRID_W:ws * GRID_W + nk, :]
        s1 = _dot_nt(q, kw) * scale + bias_ref[tid, 0]
        s2 = _dot_nt(q, kc) * scale
        m = jnp.maximum(jnp.max(s1, axis=-1, keepdims=True), jnp.max(s2, axis=-1, keepdims=True))
        p1 = jnp.exp(s1 - m)
        p2 = jnp.exp(s2 - m)
        den = jnp.sum(p1, axis=-1, keepdims=True) + jnp.sum(p2, axis=-1, keepdims=True)
        o = (_dot(p1.astype(BF16), vw) + _dot(p2.astype(BF16), vc)) / den
        o_ref[0, r0 * GRID_W:r0 * GRID_W + nq, :] = o.astype(BF16)
    s = _dot_nt(qc_ref[0], kc) * scale
    p = jnp.exp(s - jnp.max(s, axis=-1, keepdims=True))
    oc = _dot(p.astype(BF16), vc) / jnp.sum(p, axis=-1, keepdims=True)
    oc_ref[0] = oc.astype(BF16)


def _na_attention(qkv_l, qkv_c, bias, heads):
    bsz, seq, _ = qkv_l.shape
    lc = qkv_c.shape[1]
    hd = NA_HEAD_DIM
    plan, _, wr = _na_groups(seq // GRID_W)
    n_types = bias.shape[0]
    spec = lambda t, off: pl.BlockSpec((1, t, hd), lambda b, h: (b, 0, off + h))
    return pl.pallas_call(
        functools.partial(_na_kernel, plan=plan, wr=wr, scale=hd ** -0.5),
        grid=(bsz, heads),
        in_specs=[spec(seq, 0), spec(seq, heads), spec(seq, 2 * heads),
                  spec(lc, 0), spec(lc, heads), spec(lc, 2 * heads),
                  pl.BlockSpec((n_types, 1) + bias.shape[2:], lambda b, h: (0, h, 0, 0))],
        out_specs=[pl.BlockSpec((1, seq, hd), lambda b, h: (b, 0, h)),
                   pl.BlockSpec((1, lc, hd), lambda b, h: (b, 0, h))],
        out_shape=[jax.ShapeDtypeStruct((bsz, seq, heads * hd), BF16),
                   jax.ShapeDtypeStruct((bsz, lc, heads * hd), BF16)],
        compiler_params=_cp(("parallel", "arbitrary")),
    )(qkv_l, qkv_l, qkv_l, qkv_c, qkv_c, qkv_c, bias)


def _dwconv3(load, t_len, r0, rows, w, b):
    main = load(r0, rows)
    idx = lax.broadcasted_iota(jnp.int32, main.shape, 0)
    prev = load(r0 - 1, 1) if r0 > 0 else jnp.zeros_like(main[0:1])
    nxt = load(r0 + rows, 1) if r0 + rows < t_len else jnp.zeros_like(main[0:1])
    up = jnp.where(idx == 0, prev, pltpu.roll(main, 1, 0))
    down = jnp.where(idx == rows - 1, nxt, pltpu.roll(main, rows - 1, 0))
    return w[0:1] * up + w[1:2] * main + w[2:3] * down + b


CONV_ROWS = 256


def _ssd_kernel(xc_ref, bc_ref, cc_ref, dtc_ref, xl_ref, bl_ref, cl_ref, dtl_ref,
                wx_ref, wb_ref, wc_ref, bx_ref, bb_ref, bcb_ref, dtb_ref, a_ref, dsk_ref,
                y_ref, xs_ref, bs_ref, cs_ref, dts_ref, st_ref, *, lc, seq, e_n):
    q = SSD_CHUNK
    p = SSD_HEAD_DIM
    ncc, ncl = lc // q, seq // q

    for src, dst, w_ref, b_ref in ((0, xs_ref, wx_ref, bx_ref), (1, bs_ref, wb_ref, bb_ref), (2, cs_ref, wc_ref, bcb_ref)):
        w = w_ref[...]
        b = b_ref[...]
        for base, t_len, ref in ((0, lc, (xc_ref, bc_ref, cc_ref)[src]), (lc, seq, (xl_ref, bl_ref, cl_ref)[src])):
            cr = min(CONV_ROWS, t_len)
            for r0 in range(0, t_len, cr):
                dst[base + r0:base + r0 + cr, :] = _silu(
                    _dwconv3(lambda a, n, ref=ref: ref[0, a:a + n, :], t_len, r0, cr, w, b))
    for base, t_len, ref in ((0, lc, dtc_ref), (lc, seq, dtl_ref)):
        v = ref[0] + dtb_ref[...]
        dts_ref[base:base + t_len, :] = jnp.maximum(v, 0.0) + jnp.log(1.0 + jnp.exp(-jnp.abs(v)))

    ii = lax.broadcasted_iota(jnp.int32, (q, q), 0)
    jj = lax.broadcasted_iota(jnp.int32, (q, q), 1)
    a_row = a_ref[...]
    dsk = dsk_ref[...]

    for k in range(2):
        keep = (jj <= ii) if k == 0 else (jj >= ii)
        tri = jnp.where(keep, 1.0, 0.0).astype(BF16)
        st_ref[...] = jnp.zeros_like(st_ref)

        def chunk(r0, out_row, k=k, keep=keep, tri=tri):
            x = xs_ref[pl.ds(r0, q), :]
            bm = bs_ref[pl.ds(r0, q), :]
            cm = cs_ref[pl.ds(r0, q), :].astype(BF16)
            dt = dts_ref[pl.ds(r0, q), :]
            dta = dt * a_row
            d_hi = dta.astype(BF16)
            r1 = dta - d_hi.astype(F32)
            d_mid = r1.astype(BF16)
            d_lo = (r1 - d_mid.astype(F32)).astype(BF16)
            cum = _dot(tri, d_hi) + (_dot(tri, d_mid) + _dot(tri, d_lo))
            cum_t = cum.T
            tot = cum[q - 1:q, :] if k == 0 else cum[0:1, :]
            bm_t = bm.T.astype(BF16)
            s_prev = st_ref[...]
            xw, ydiag, eoff, dec = [], [], [], []
            if out_row is not None:
                gmat = _dot_nt(cm, bm.astype(BF16))
            for e in range(e_n):
                col = k * e_n + e
                ccol = cum[:, col:col + 1]
                xdt = x[:, e * p:(e + 1) * p] * dt[:, col:col + 1]
                xw.append(xdt * jnp.exp(tot[:, col:col + 1] - ccol))
                dec.append(jnp.broadcast_to(jnp.exp(tot[:, col:col + 1]), (1, p)))
                if out_row is not None:
                    seg = jnp.where(keep, ccol - cum_t[col:col + 1, :], NEG)
                    ydiag.append(_dot((gmat * jnp.exp(seg)).astype(BF16), xdt.astype(BF16)))
                    eoff.append(jnp.broadcast_to(jnp.exp(ccol), (q, p)))
            st_ref[...] = s_prev * jnp.concatenate(dec, axis=1) + _dot(bm_t, jnp.concatenate(xw, axis=1).astype(BF16))
            if out_row is not None:
                y = jnp.concatenate(ydiag, axis=1) + _dot(cm, s_prev.astype(BF16)) * jnp.concatenate(eoff, axis=1)
                if k == 0:
                    y_ref[0, pl.ds(out_row, q), :] = y + dsk * x
                else:
                    y_ref[0, pl.ds(out_row, q), :] += y

        def ctx_body(i, c):
            cid = i if k == 0 else ncc - 1 - i
            chunk(pl.multiple_of(cid * q, q), None)
            return c

        def lat_body(i, c):
            cid = i if k == 0 else ncl - 1 - i
            chunk(pl.multiple_of(lc + cid * q, q), pl.multiple_of(cid * q, q))
            return c

        lax.fori_loop(0, ncc, ctx_body, 0)
        lax.fori_loop(0, ncl, lat_body, 0)


def _ssd(p_l, dt_l, p_c, dt_c, x_off_l, x_off_c, conv_w, conv_b, dtb, a_neg, d_exp, inner):
    bsz, seq, _ = p_l.shape
    lc = p_c.shape[1]
    e_n = inner // SSD_HEAD_DIM // SSD_GROUPS
    xw = e_n * SSD_HEAD_DIM
    n = SSD_STATE
    t_tot = lc + seq

    def specs(t, off):
        return [pl.BlockSpec((1, t, xw), lambda b, g: (b, 0, off // xw + g)),
                pl.BlockSpec((1, t, n), lambda b, g: (b, 0, (off + inner) // n + g)),
                pl.BlockSpec((1, t, n), lambda b, g: (b, 0, (off + inner) // n + SSD_GROUPS + g)),
                pl.BlockSpec((1, t, V7X_LANES), lambda b, g: (b, 0, g))]

    def pspecs(rows):
        return [pl.BlockSpec((rows, xw), lambda b, g: (0, g)),
                pl.BlockSpec((rows, n), lambda b, g: (0, inner // n + g)),
                pl.BlockSpec((rows, n), lambda b, g: (0, inner // n + SSD_GROUPS + g))]

    lane = lambda: pl.BlockSpec((1, V7X_LANES), lambda b, g: (0, g))
    return pl.pallas_call(
        functools.partial(_ssd_kernel, lc=lc, seq=seq, e_n=e_n),
        grid=(bsz, SSD_GROUPS),
        in_specs=specs(lc, x_off_c) + specs(seq, x_off_l) + pspecs(3) + pspecs(1)
        + [lane(), lane(), pl.BlockSpec((1, xw), lambda b, g: (0, g))],
        out_specs=pl.BlockSpec((1, seq, xw), lambda b, g: (b, 0, g)),
        out_shape=jax.ShapeDtypeStruct((bsz, seq, inner), F32),
        scratch_shapes=[pltpu.VMEM((t_tot, xw), F32), pltpu.VMEM((t_tot, n), F32), pltpu.VMEM((t_tot, n), F32),
                        pltpu.VMEM((t_tot, V7X_LANES), F32), pltpu.VMEM((n, xw), F32)],
        compiler_params=_cp(("parallel", "arbitrary")),
    )(p_c, p_c, p_c, dt_c, p_l, p_l, p_l, dt_l, conv_w, conv_w, conv_w,
      conv_b, conv_b, conv_b, dtb, a_neg, d_exp)


def _hy_filter_kernel(z_ref, w_in_ref, b_in_ref, w_mid_ref, b_mid_ref, w_out_ref, freq_ref, delta_ref, o_ref, h_ref):
    j = pl.program_id(1)
    hp = lax.Precision.HIGHEST

    @pl.when(j == 0)
    def _():
        freq = freq_ref[...]
        h = jnp.sin(freq * (jnp.dot(z_ref[...], w_in_ref[...], precision=hp, preferred_element_type=F32) + b_in_ref[...]))
        for i in range(w_mid_ref.shape[0]):
            h = jnp.sin(freq * (jnp.dot(h, w_mid_ref[i], precision=hp, preferred_element_type=F32) + b_mid_ref[i]))
        h_ref[...] = h

    decay = jnp.exp(-z_ref[:, 0:1] * delta_ref[...])
    o_ref[...] = jnp.dot(h_ref[...], w_out_ref[...], precision=hp, preferred_element_type=F32) * decay


def _hy_filters(seq, w_in, b_in, w_mid, b_mid, w_out, freq, width):
    t = jnp.linspace(0.0, 1.0, seq, dtype=F32)[:, None]
    w = 2.0 * math.pi * jnp.arange(seq, dtype=F32)[:, None] / seq
    f = jnp.linspace(1e-4, HY_BANDS - 1, HY_BANDS, dtype=F32)[None, :]
    z = jnp.concatenate([t, jnp.cos(f * w), -jnp.sin(f * w)], axis=-1)
    emb, hid = w_in.shape
    pad = V7X_LANES
    z = jnp.pad(z, ((0, 0), (0, pad - emb)))
    w_in_p = jnp.pad(w_in.astype(F32), ((0, pad - emb), (0, pad - hid)))
    padv = lambda v: jnp.pad(v.astype(F32), ((0, 0), (0, pad - hid)))
    w_mid_p = jnp.pad(w_mid.astype(F32), ((0, 0), (0, pad - hid), (0, pad - hid)))
    w_out_p = jnp.pad(w_out.astype(F32), ((0, pad - hid), (0, 0)))
    n_out = w_out.shape[1]
    max_decay = math.log(HY_DECAY_TARGET) / HY_FAST_PCT
    min_decay = math.log(HY_DECAY_TARGET) / HY_SLOW_PCT
    deltas = jnp.abs(jnp.linspace(min_decay, max_decay, width, dtype=F32))[None, :]
    tr = 256
    n_mid = w_mid.shape[0]
    return pl.pallas_call(
        _hy_filter_kernel,
        grid=(seq // tr, n_out // width),
        in_specs=[pl.BlockSpec((tr, pad), lambda i, j: (i, 0)),
                  pl.BlockSpec((pad, pad), lambda i, j: (0, 0)),
                  pl.BlockSpec((1, pad), lambda i, j: (0, 0)),
                  pl.BlockSpec((n_mid, pad, pad), lambda i, j: (0, 0, 0)),
                  pl.BlockSpec((n_mid, 1, pad), lambda i, j: (0, 0, 0)),
                  pl.BlockSpec((pad, width), lambda i, j: (0, j)),
                  pl.BlockSpec((1, pad), lambda i, j: (0, 0)),
                  pl.BlockSpec((1, width), lambda i, j: (0, 0))],
        out_specs=pl.BlockSpec((tr, width), lambda i, j: (i, j)),
        out_shape=jax.ShapeDtypeStruct((seq, n_out), F32),
        scratch_shapes=[pltpu.VMEM((tr, pad), F32)],
        compiler_params=_cp(("parallel", "arbitrary")),
    )(z, w_in_p, padv(b_in[None]), w_mid_p, padv(b_mid)[:, None, :], w_out_p, padv(freq[None]), deltas)


def _dft_tables(seq):
    n = 2 * seq
    k = jnp.arange(seq, dtype=jnp.int32)
    ang = (2.0 * math.pi / n) * ((k[:, None] * k[None, :]) % n).astype(F32)
    cos = jnp.cos(ang)
    sin = jnp.sin(ang)
    alt = jnp.where(k % 2 == 0, 1.0, -1.0).astype(F32)
    sf = jnp.where(k[:, None] == 0, alt[None, :], sin)
    wk = jnp.where(k == 0, 1.0, 2.0).astype(F32) / n
    ci = cos * wk[None, :]
    si = jnp.where(k[None, :] == 0, alt[:, None] / n, sin * (2.0 / n))
    return tuple(p for m in (cos, sf, ci, si) for p in _split2(m))


def _hy_spec_kernel(hf_ref, hb_ref, cfh_ref, cfl_ref, sfh_ref, sfl_ref, kr_ref, ki_ref,
                    smh_ref, sml_ref, dfh_ref, dfl_ref, nyq_ref):
    k = pl.program_id(1)

    @pl.when(k == 0)
    def _():
        hf = hf_ref[...]
        t = lax.broadcasted_iota(jnp.int32, hf.shape, 0)
        hb = jnp.where(t == 0, 0.0, hb_ref[...])
        sm = hf + hb
        df = hf - hb
        smh_ref[...], sml_ref[...] = _split2(sm)
        dfh_ref[...], dfl_ref[...] = _split2(df)
        nyq_ref[...] = jnp.sum(jnp.where(t % 2 == 0, sm, -sm), axis=0, keepdims=True)

    kr_ref[...] = _dot3(cfh_ref[...], cfl_ref[...], smh_ref[...], sml_ref[...])
    ki = _dot3(sfh_ref[...], sfl_ref[...], dfh_ref[...], dfl_ref[...])
    row = lax.broadcasted_iota(jnp.int32, ki.shape, 0)
    ki_ref[...] = jnp.where((row == 0) & (k == 0), nyq_ref[...], ki)


def _hy_spectra(h, tabs, width):
    seq, n_cols = h.shape
    order = n_cols // (2 * width)
    cpb = width // HY_TC
    cfh, cfl, sfh, sfl = tabs[:4]
    fspec = lambda: pl.BlockSpec((HY_TK, seq), lambda c, k: (k, 0))
    return pl.pallas_call(
        _hy_spec_kernel,
        grid=(order * cpb, seq // HY_TK),
        in_specs=[pl.BlockSpec((seq, HY_TC), lambda c, k: (0, (c // cpb) * 2 * cpb + c % cpb)),
                  pl.BlockSpec((seq, HY_TC), lambda c, k: (0, (c // cpb) * 2 * cpb + cpb + c % cpb)),
                  fspec(), fspec(), fspec(), fspec()],
        out_specs=[pl.BlockSpec((HY_TK, HY_TC), lambda c, k: (k, c)),
                   pl.BlockSpec((HY_TK, HY_TC), lambda c, k: (k, c))],
        out_shape=[jax.ShapeDtypeStruct((seq, order * width), F32)] * 2,
        scratch_shapes=[pltpu.VMEM((seq, HY_TC), BF16)] * 4 + [pltpu.VMEM((1, HY_TC), F32)],
        compiler_params=_cp(("parallel", "arbitrary")),
    )(h, h, cfh, cfl, sfh, sfl)


def _hy_conv_kernel(u_ref, g_ref, wu_ref, bu_ref, wg_ref, bg_ref, kr_ref, ki_ref, fb_ref,
                    cfh_ref, cfl_ref, sfh_ref, sfl_ref, cih_ref, cil_ref, sih_ref, sil_ref,
                    o_ref, uh_ref, ul_ref, acc_ref, *, conv_u, seq):
    k = pl.program_id(2)
    cr = CONV_ROWS

    def u_rows(r0):
        if conv_u:
            return _dwconv3(lambda a, n: u_ref[0, a:a + n, :], seq, r0, cr, wu_ref[...], bu_ref[...])
        return u_ref[0, r0:r0 + cr, :]

    @pl.when(k == 0)
    def _():
        for r0 in range(0, seq, cr):
            uh_ref[r0:r0 + cr, :], ul_ref[r0:r0 + cr, :] = _split2(u_rows(r0))
        acc_ref[...] = jnp.zeros_like(acc_ref)

    uh = uh_ref[...]
    ul = ul_ref[...]
    a = _dot3(cfh_ref[...], cfl_ref[...], uh, ul)
    b = _dot3(sfh_ref[...], sfl_ref[...], uh, ul)
    kr = kr_ref[...]
    ki = ki_ref[...]
    first = (lax.broadcasted_iota(jnp.int32, a.shape, 0) == 0) & (k == 0)
    bki = b * ki
    yc = a * kr - jnp.where(first, 0.0, bki)
    ys = jnp.where(first, bki, a * ki + b * kr)
    ych, ycl = _split2(yc)
    ysh, ysl = _split2(ys)
    acc_ref[...] += _dot3(cih_ref[...], cil_ref[...], ych, ycl) + _dot3(sih_ref[...], sil_ref[...], ysh, ysl)

    @pl.when(k == pl.num_programs(2) - 1)
    def _():
        for r0 in range(0, seq, cr):
            gate = _dwconv3(lambda a_, n: g_ref[0, a_:a_ + n, :], seq, r0, cr, wg_ref[...], bg_ref[...])
            o_ref[0, r0:r0 + cr, :] = gate * (acc_ref[r0:r0 + cr, :] + u_rows(r0) * fb_ref[0])


def _hy_conv(u, u_blk, g, g_blk, short_w, short_b, wu_blk, wg_blk, kr, ki, k_blk, fbias, order, tabs, width, conv_u):
    bsz, seq, _ = g.shape
    cpb = width // HY_TC
    fspec = lambda: pl.BlockSpec((HY_TK, seq), lambda b, c, k: (k, 0))
    ispec = lambda: pl.BlockSpec((seq, HY_TK), lambda b, c, k: (0, k))
    once = dict(pipeline_mode=pl.Buffered(1))
    return pl.pallas_call(
        functools.partial(_hy_conv_kernel, conv_u=conv_u, seq=seq),
        grid=(bsz, cpb, seq // HY_TK),
        in_specs=[pl.BlockSpec((1, seq, HY_TC), lambda b, c, k: (b, 0, u_blk + c), **once),
                  pl.BlockSpec((1, seq, HY_TC), lambda b, c, k: (b, 0, g_blk + c), **once),
                  pl.BlockSpec((3, HY_TC), lambda b, c, k: (0, wu_blk + c)),
                  pl.BlockSpec((1, HY_TC), lambda b, c, k: (0, wu_blk + c)),
                  pl.BlockSpec((3, HY_TC), lambda b, c, k: (0, wg_blk + c)),
                  pl.BlockSpec((1, HY_TC), lambda b, c, k: (0, wg_blk + c)),
                  pl.BlockSpec((HY_TK, HY_TC), lambda b, c, k: (k, k_blk + c)),
                  pl.BlockSpec((HY_TK, HY_TC), lambda b, c, k: (k, k_blk + c)),
                  pl.BlockSpec((1, 1, HY_TC), lambda b, c, k: (order, 0, c)),
                  fspec(), fspec(), fspec(), fspec(), ispec(), ispec(), ispec(), ispec()],
        out_specs=pl.BlockSpec((1, seq, HY_TC), lambda b, c, k: (b, 0, c)),
        out_shape=jax.ShapeDtypeStruct((bsz, seq, width), F32),
        scratch_shapes=[pltpu.VMEM((seq, HY_TC), BF16), pltpu.VMEM((seq, HY_TC), BF16), pltpu.VMEM((seq, HY_TC), F32)],
        compiler_params=_cp(("parallel", "parallel", "arbitrary")),
    )(u, g, short_w, short_b, short_w, short_b, kr, ki, fbias, *tabs)


def kernel(x, c, ctx, c_ctx, mod_w, mod_b, norm_g, ffn_wg, ffn_wu, ffn_wd, final_g,
           ev_w_in, ev_w_out, s5_a_re, s5_a_im, s5_log_dt, s5_b_re, s5_b_im, s5_c_re, s5_c_im,
           s5_d, s5_glu_w, s5_glu_b, na_rpb,
           od_w_in, od_w_out, hy_short_w, hy_short_b, hy_w_in, hy_b_in, hy_w_mid, hy_b_mid,
           hy_w_out, hy_freq, hy_fbias,
           ssd_conv_w, ssd_conv_b, ssd_dt_bias, ssd_a_log, ssd_d, ssd_norm_g):
    bsz, seq, d = x.shape
    lc = ctx.shape[1]
    depth = mod_w.shape[0]
    assert depth == 2 and bsz == V7X_SUBLANES, "layer 0 = S5 || attention with context output, layer 1 = Hyena || SSD"
    assert seq % TM == 0 and (bsz * lc) % TM == 0 and lc % CONV_ROWS == 0

    xl = x.reshape(bsz * seq, d)
    xc = ctx.reshape(bsz * lc, d)
    tps = seq // TM
    mod_l = lambda i: i // tps
    mod_c = lambda i: bsz

    cond = jnp.concatenate([c, c_ctx[None], jnp.zeros((V7X_SUBLANES - 1, d), F32)], axis=0)
    mods = _modulation(cond, mod_w, mod_b).reshape(depth, cond.shape[0], N_MOD, d)

    wg, wu, wd = ffn_wg.astype(BF16), ffn_wu.astype(BF16), ffn_wd.astype(BF16)

    def ffn(xx, layer, half, mod_of, final=False):
        return _ffn(xx, mods[layer], mod_of, norm_g[layer, 2 * half], wg[layer, half], wu[layer, half], wd[layer, half],
                    final_g, 2 * half, final, TM)

    xl = ffn(xl, 0, 0, mod_l)
    xc = ffn(xc, 0, 0, mod_c)
    w_in = ev_w_in[0].astype(BF16)
    s5w = s5_d.shape[1] * s5_d.shape[2]
    u_l, qkv_l = _inproj_even(xl, mods[0], mod_l, norm_g[0, 1], w_in, seq, bsz, TM)
    u_c, qkv_c = _inproj_even(xc, mods[0], mod_c, norm_g[0, 1], w_in, lc, bsz, lc)
    u_l = u_l.reshape(seq * bsz, s5w)
    u_c = u_c.reshape(lc * bsz, s5w)
    s5p = _s5_params(s5_a_re[0], s5_a_im[0], s5_log_dt[0], s5_b_re[0], s5_b_im[0], s5_c_re[0], s5_c_im[0])
    y_c, y_l = _s5_scan(u_c, u_l, *s5p, bsz)
    glu_w = s5_glu_w[0].astype(BF16)
    a_l = _s5_glu(u_l, y_l, s5_d[0], glu_w, s5_glu_b[0], seq, bsz, TM)
    a_c = _s5_glu(u_c, y_c, s5_d[0], glu_w, s5_glu_b[0], lc, bsz, lc)
    heads = na_rpb.shape[1]
    bias = _na_bias(na_rpb[0], seq // GRID_W)
    o_l, o_c = _na_attention(qkv_l.reshape(bsz, seq, -1), qkv_c.reshape(bsz, lc, -1), bias, heads)
    w_out = ev_w_out[0].astype(BF16)
    xl = _outproj_even(a_l, o_l.reshape(bsz * seq, -1), xl, mods[0], mod_l, w_out, TM)
    xc = _outproj_even(a_c, o_c.reshape(bsz * lc, -1), xc, mods[0], mod_c, w_out, TM)
    xl = ffn(xl, 0, 1, mod_l)
    xc = ffn(xc, 0, 1, mod_c)

    xl = ffn(xl, 1, 0, mod_l)
    xc = ffn(xc, 1, 0, mod_c)
    hyw = hy_fbias.shape[2]
    inner = ssd_norm_g.shape[1]
    n_heads = ssd_d.shape[1]
    e_n = n_heads // SSD_GROUPS
    o_z = 3 * hyw
    o_xbc = o_z + inner
    o_dt = o_xbc + inner + 2 * SSD_GROUPS * SSD_STATE
    w_od = od_w_in[0]
    w_dt = w_od[:, o_dt:].reshape(d, 2, SSD_GROUPS, e_n)
    w_dt = jnp.transpose(w_dt, (0, 2, 1, 3)).reshape(d, SSD_GROUPS, 2 * e_n)
    w_dt = jnp.pad(w_dt, ((0, 0), (0, 0), (0, V7X_LANES - 2 * e_n))).reshape(d, SSD_GROUPS * V7X_LANES).astype(BF16)

    def regroup(v):
        v = jnp.transpose(v.astype(F32).reshape(2, SSD_GROUPS, e_n), (1, 0, 2)).reshape(SSD_GROUPS, 2 * e_n)
        return jnp.pad(v, ((0, 0), (0, V7X_LANES - 2 * e_n))).reshape(1, SSD_GROUPS * V7X_LANES)

    w_main = w_od[:, :o_dt].astype(BF16)
    p_l, dt_l = _inproj_odd(xl, mods[1], mod_l, norm_g[1, 1], w_main, w_dt, TM)
    p_c, dt_c = _inproj_odd(xc, mods[1], mod_c, norm_g[1, 1], w_main[:, o_xbc:], w_dt, TM)
    p_l3 = p_l.reshape(bsz, seq, -1)
    y_ssd = _ssd(p_l3, dt_l.reshape(bsz, seq, -1), p_c.reshape(bsz, lc, -1), dt_c.reshape(bsz, lc, -1),
                 o_xbc, 0, ssd_conv_w[0], ssd_conv_b[0][None], regroup(ssd_dt_bias[0]),
                 regroup(-jnp.exp(ssd_a_log[0].astype(F32))),
                 jnp.repeat(ssd_d[0].astype(F32), SSD_HEAD_DIM)[None], inner)

    filt = _hy_filters(seq, hy_w_in[0], hy_b_in[0], hy_w_mid[0], hy_b_mid[0], hy_w_out[0], hy_freq[0], hyw)
    tabs = _dft_tables(seq)
    kr, ki = _hy_spectra(filt, tabs, hyw)
    cpb = hyw // HY_TC
    sw, sb = hy_short_w[0], hy_short_b[0][None]
    fb = hy_fbias[0][:, None, :]
    z1 = _hy_conv(p_l3, 2 * cpb, p_l3, 0, sw, sb, 2 * cpb, 0, kr, ki, 0, fb, 0, tabs, hyw, True)
    y_hy = _hy_conv(z1, 0, p_l3, cpb, sw, sb, 0, cpb, kr, ki, cpb, fb, 1, tabs, hyw, False)

    xl = _outproj_odd(y_hy.reshape(bsz * seq, hyw), y_ssd.reshape(bsz * seq, inner), p_l, o_z // inner,
                      ssd_norm_g[0], xl, mods[1], mod_l, od_w_out[0].astype(BF16), TM)
    xl = ffn(xl, 1, 1, mod_l, final=True)
    return xl.reshape(bsz, seq, d)
```

```python
import functools
import math

import numpy as np
import jax
import jax.numpy as jnp
from jax import lax
from jax.experimental import pallas as pl
from jax.experimental.pallas import tpu as pltpu

F32 = jnp.float32
BF16 = jnp.bfloat16

EPS = 1e-6
GRID_W = 64
N_MOD = 9
S5_GROUP = 16
S5_STATE = 64
NA_HEAD_DIM = 128
NA_KH = 8
NA_KW = 16
HY_BANDS = 16
HY_DECAY_TARGET = 1e-2
HY_FAST_PCT = 0.3
HY_SLOW_PCT = 1.5
SSD_HEAD_DIM = 64
SSD_GROUPS = 4
SSD_STATE = 128
SSD_CHUNK = 128

V7X_LANES = 128
V7X_SUBLANES = 8
V7X_VMEM_BYTES = 64 * 1024 * 1024
VMEM_LIMIT = V7X_VMEM_BYTES - 12 * 1024 * 1024

TM = 512
TF = 512
TN = 1024
S5_TC = 32
S5_W = 512
NA_GQ = 4
HY_TC = 512
HY_TK = 256
NEG = -1e30


def _cp(sem, vmem=VMEM_LIMIT):
    return pltpu.CompilerParams(dimension_semantics=sem, vmem_limit_bytes=vmem)


def _sigmoid(x):
    return 1.0 / (1.0 + jnp.exp(-x))


def _silu(x):
    return x * _sigmoid(x)


def _dot(a, b):
    return jnp.dot(a, b, preferred_element_type=F32)


def _dot_nt(a, b):
    return lax.dot_general(a, b, (((1,), (1,)), ((), ())), preferred_element_type=F32)


def _split2(x):
    hi = x.astype(BF16)
    lo = (x - hi.astype(F32)).astype(BF16)
    return hi, lo


def _dot3(a_hi, a_lo, b_hi, b_lo):
    return _dot(a_hi, b_hi) + (_dot(a_hi, b_lo) + _dot(a_lo, b_hi))


def _prenorm(x, g, shift, scale):
    y = x * lax.rsqrt(jnp.mean(x * x, axis=-1, keepdims=True) + EPS)
    return y * g * (1.0 + scale) + shift


def _mod_kernel(c_ref, w_ref, b_ref, o_ref):
    cs = _silu(c_ref[...]).astype(BF16)
    o_ref[0] = _dot(cs, w_ref[0].astype(BF16)) + b_ref[0]


def _modulation(cond, mod_w, mod_b):
    depth, d, n = mod_w.shape
    rows = cond.shape[0]
    return pl.pallas_call(
        _mod_kernel,
        grid=(depth, n // TN),
        in_specs=[pl.BlockSpec((rows, d), lambda l, j: (0, 0)),
                  pl.BlockSpec((1, d, TN), lambda l, j: (l, 0, j)),
                  pl.BlockSpec((1, 1, TN), lambda l, j: (l, 0, j))],
        out_specs=pl.BlockSpec((1, rows, TN), lambda l, j: (l, 0, j)),
        out_shape=jax.ShapeDtypeStruct((depth, rows, n), F32),
        compiler_params=_cp(("arbitrary", "arbitrary")),
        name="adaln_mod",
    )(cond, mod_w, mod_b.reshape(depth, 1, n))


def _ffn_kernel(x_ref, mod_ref, g_ref, wg_ref, wu_ref, wd_ref, fg_ref, o_ref, xn_ref, acc_ref, *, sub, final):
    f = pl.program_id(1)

    @pl.when(f == 0)
    def _():
        xn = _prenorm(x_ref[...], g_ref[...], mod_ref[0, 3 * sub:3 * sub + 1, :], mod_ref[0, 3 * sub + 1:3 * sub + 2, :])
        xn_ref[...] = xn.astype(BF16)
        acc_ref[...] = jnp.zeros_like(acc_ref)

    xn = xn_ref[...]
    h = (_silu(_dot(xn, wg_ref[...])) * _dot(xn, wu_ref[...])).astype(BF16)
    acc_ref[...] += _dot(h, wd_ref[...])

    @pl.when(f == pl.num_programs(1) - 1)
    def _():
        y = x_ref[...] + 0.5 * mod_ref[0, 3 * sub + 2:3 * sub + 3, :] * acc_ref[...]
        if final:
            y = y * lax.rsqrt(jnp.mean(y * y, axis=-1, keepdims=True) + EPS) * fg_ref[...]
        o_ref[...] = y


def _ffn(x, mods, mod_of_tile, norm_g, wg, wu, wd, final_g, sub, final, tm):
    n_tok, d = x.shape
    ff = wg.shape[1]
    return pl.pallas_call(
        functools.partial(_ffn_kernel, sub=sub, final=final),
        grid=(n_tok // tm, ff // TF),
        in_specs=[pl.BlockSpec((tm, d), lambda i, f: (i, 0)),
                  pl.BlockSpec((1, N_MOD, d), lambda i, f: (mod_of_tile(i), 0, 0)),
                  pl.BlockSpec((1, d), lambda i, f: (0, 0)),
                  pl.BlockSpec((d, TF), lambda i, f: (0, f)),
                  pl.BlockSpec((d, TF), lambda i, f: (0, f)),
                  pl.BlockSpec((TF, d), lambda i, f: (f, 0)),
                  pl.BlockSpec((1, d), lambda i, f: (0, 0))],
        out_specs=pl.BlockSpec((tm, d), lambda i, f: (i, 0)),
        out_shape=jax.ShapeDtypeStruct((n_tok, d), F32),
        scratch_shapes=[pltpu.VMEM((tm, d), BF16), pltpu.VMEM((tm, d), F32)],
        compiler_params=_cp(("parallel", "arbitrary")),
        name="swiglu",
    )(x, mods, norm_g.reshape(1, d), wg, wu, wd, final_g.reshape(1, d))


def _inproj_even_kernel(x_ref, mod_ref, g_ref, w_ref, u_ref, qkv_ref, xn_ref):
    n = pl.program_id(1)

    @pl.when(n == 0)
    def _():
        xn = _prenorm(x_ref[...], g_ref[...], mod_ref[0, 3:4, :], mod_ref[0, 4:5, :])
        xn_ref[...] = xn.astype(BF16)

    res = _dot(xn_ref[...], w_ref[...])

    @pl.when(n == 0)
    def _():
        u_ref[...] = res

    @pl.when(n > 0)
    def _():
        qkv_ref[...] = res.astype(BF16)


def _inproj_even(x, mods, mod_of_tile, norm_g, w, seq, nb, tm):
    n_tok, d = x.shape
    n_in = w.shape[1]
    s5w = TN
    tps = seq // tm
    return pl.pallas_call(
        _inproj_even_kernel,
        grid=(n_tok // tm, n_in // TN),
        in_specs=[pl.BlockSpec((tm, d), lambda i, n: (i, 0)),
                  pl.BlockSpec((1, N_MOD, d), lambda i, n: (mod_of_tile(i), 0, 0)),
                  pl.BlockSpec((1, d), lambda i, n: (0, 0)),
                  pl.BlockSpec((d, TN), lambda i, n: (0, n))],
        out_specs=[pl.BlockSpec((tm, s5w), lambda i, n: (i % tps, i // tps)),
                   pl.BlockSpec((tm, TN), lambda i, n: (i, jnp.maximum(n - 1, 0)))],
        out_shape=[jax.ShapeDtypeStruct((seq, nb * s5w), F32),
                   jax.ShapeDtypeStruct((n_tok, n_in - s5w), BF16)],
        scratch_shapes=[pltpu.VMEM((tm, d), BF16)],
        compiler_params=_cp(("parallel", "arbitrary")),
        name="inproj_even",
    )(x, mods, norm_g.reshape(1, d), w)


def _inproj_odd_kernel(x_ref, mod_ref, g_ref, w_ref, wdt_ref, p_ref, dt_ref, xn_ref):
    n = pl.program_id(1)

    @pl.when(n == 0)
    def _():
        xn = _prenorm(x_ref[...], g_ref[...], mod_ref[0, 3:4, :], mod_ref[0, 4:5, :])
        xn_ref[...] = xn.astype(BF16)
        dt_ref[...] = _dot(xn_ref[...], wdt_ref[...])

    p_ref[...] = _dot(xn_ref[...], w_ref[...])


def _inproj_odd(x, mods, mod_of_tile, norm_g, w, wdt, tm):
    n_tok, d = x.shape
    n_in = w.shape[1]
    n_dt = wdt.shape[1]
    return pl.pallas_call(
        _inproj_odd_kernel,
        grid=(n_tok // tm, n_in // TN),
        in_specs=[pl.BlockSpec((tm, d), lambda i, n: (i, 0)),
                  pl.BlockSpec((1, N_MOD, d), lambda i, n: (mod_of_tile(i), 0, 0)),
                  pl.BlockSpec((1, d), lambda i, n: (0, 0)),
                  pl.BlockSpec((d, TN), lambda i, n: (0, n)),
                  pl.BlockSpec((d, n_dt), lambda i, n: (0, 0))],
        out_specs=[pl.BlockSpec((tm, TN), lambda i, n: (i, n)),
                   pl.BlockSpec((tm, n_dt), lambda i, n: (i, 0))],
        out_shape=[jax.ShapeDtypeStruct((n_tok, n_in), F32),
                   jax.ShapeDtypeStruct((n_tok, n_dt), F32)],
        scratch_shapes=[pltpu.VMEM((tm, d), BF16)],
        compiler_params=_cp(("parallel", "arbitrary")),
        name="inproj_odd",
    )(x, mods, norm_g.reshape(1, d), w, wdt)


def _outproj_even_kernel(a_ref, b_ref, x_ref, mod_ref, wa_ref, wb_ref, o_ref):
    y = _dot(a_ref[...], wa_ref[...]) + _dot(b_ref[...], wb_ref[...])
    o_ref[...] = x_ref[...] + mod_ref[0, 5:6, :] * y


def _outproj_even(a, b, x, mods, mod_of_tile, w, tm):
    n_tok, d = x.shape
    half = a.shape[1]
    kb = half // TN
    return pl.pallas_call(
        _outproj_even_kernel,
        grid=(n_tok // tm, d // TN),
        in_specs=[pl.BlockSpec((tm, half), lambda i, n: (i, 0)),
                  pl.BlockSpec((tm, half), lambda i, n: (i, 0)),
                  pl.BlockSpec((tm, TN), lambda i, n: (i, n)),
                  pl.BlockSpec((1, N_MOD, TN), lambda i, n: (mod_of_tile(i), 0, n)),
                  pl.BlockSpec((half, TN), lambda i, n: (0, n)),
                  pl.BlockSpec((half, TN), lambda i, n: (kb, n))],
        out_specs=pl.BlockSpec((tm, TN), lambda i, n: (i, n)),
        out_shape=jax.ShapeDtypeStruct((n_tok, d), F32),
        compiler_params=_cp(("parallel", "arbitrary")),
        name="outproj_even",
    )(a, b, x, mods, w, w)


def _outproj_odd_kernel(hy_ref, ys_ref, z_ref, ng_ref, x_ref, mod_ref, wa_ref, wb_ref, o_ref, b_ref):
    n = pl.program_id(1)

    @pl.when(n == 0)
    def _():
        y = ys_ref[...] * _silu(z_ref[...])
        y = y * lax.rsqrt(jnp.mean(y * y, axis=-1, keepdims=True) + EPS) * ng_ref[...]
        b_ref[...] = y.astype(BF16)

    y = _dot(hy_ref[...].astype(BF16), wa_ref[...]) + _dot(b_ref[...], wb_ref[...])
    o_ref[...] = x_ref[...] + mod_ref[0, 5:6, :] * y


def _outproj_odd(hy, ys, p, z_block, norm_g, x, mods, mod_of_tile, w, tm):
    n_tok, d = x.shape
    half = hy.shape[1]
    kb = half // TN
    return pl.pallas_call(
        _outproj_odd_kernel,
        grid=(n_tok // tm, d // TN),
        in_specs=[pl.BlockSpec((tm, half), lambda i, n: (i, 0)),
                  pl.BlockSpec((tm, half), lambda i, n: (i, 0)),
                  pl.BlockSpec((tm, half), lambda i, n: (i, z_block)),
                  pl.BlockSpec((1, half), lambda i, n: (0, 0)),
                  pl.BlockSpec((tm, TN), lambda i, n: (i, n)),
                  pl.BlockSpec((1, N_MOD, TN), lambda i, n: (mod_of_tile(i), 0, n)),
                  pl.BlockSpec((half, TN), lambda i, n: (0, n)),
                  pl.BlockSpec((half, TN), lambda i, n: (kb, n))],
        out_specs=pl.BlockSpec((tm, TN), lambda i, n: (i, n)),
        out_shape=jax.ShapeDtypeStruct((n_tok, d), F32),
        scratch_shapes=[pltpu.VMEM((tm, half), BF16)],
        compiler_params=_cp(("parallel", "arbitrary")),
        name="outproj_odd",
    )(hy, ys, p, norm_g.reshape(1, half), x, mods, w, w)


def _s5_kernel(uc_ref, ul_ref, a_ref, bre_ref, bim_ref, cre_ref, cim_ref, yc_ref, yl_ref,
               s_ref, carry_ref, ub_ref, yb_ref, *, nc_c, tc, nb, kt_n, half):
    d = pl.program_id(0)
    j = pl.program_id(1)
    kin = bre_ref.shape[2]
    kst = bre_ref.shape[3]
    ln = ub_ref.shape[2]
    w = ub_ref.shape[0] * ln
    lpk = kin // ln

    @pl.when(j == 0)
    def _():
        carry_ref[...] = jnp.zeros_like(carry_ref)

    def run(u_ref, y_ref):
        for b in range(nb):
            for cb in range(w // ln):
                ub_ref[cb, pl.ds(b, tc, stride=nb), :] = u_ref[:, b * w + cb * ln:b * w + (cb + 1) * ln]
        for kt in range(kt_n):
            uk = jnp.concatenate([ub_ref[kt * lpk + i] for i in range(lpk)], axis=1).astype(BF16)
            s_ref[:, kt * kst:(kt + 1) * kst] = _dot(uk, bre_ref[0, kt])
            s_ref[:, half + kt * kst:half + (kt + 1) * kst] = _dot(uk, bim_ref[0, kt])
        for c0 in range(0, half, S5_W):
            a_re = jnp.broadcast_to(a_ref[0, 0:1, c0:c0 + S5_W], (nb, S5_W))
            a_im = jnp.broadcast_to(a_ref[0, 1:2, c0:c0 + S5_W], (nb, S5_W))

            def step(i, st):
                s_re, s_im = st
                t = jnp.where(d == 0, i, tc - 1 - i)
                row = pl.multiple_of(t * nb, nb)
                n_re = a_re * s_re - a_im * s_im + s_ref[pl.ds(row, nb), c0:c0 + S5_W]
                n_im = a_re * s_im + a_im * s_re + s_ref[pl.ds(row, nb), half + c0:half + c0 + S5_W]
                s_ref[pl.ds(row, nb), c0:c0 + S5_W] = n_re
                s_ref[pl.ds(row, nb), half + c0:half + c0 + S5_W] = n_im
                return n_re, n_im

            s_re, s_im = lax.fori_loop(0, tc, step, (carry_ref[:, c0:c0 + S5_W], carry_ref[:, half + c0:half + c0 + S5_W]),
                                       unroll=4)
            carry_ref[:, c0:c0 + S5_W] = s_re
            carry_ref[:, half + c0:half + c0 + S5_W] = s_im
        for kt in range(kt_n):
            s_re = s_ref[:, kt * kst:(kt + 1) * kst].astype(BF16)
            s_im = s_ref[:, half + kt * kst:half + (kt + 1) * kst].astype(BF16)
            yk = _dot(s_re, cre_ref[0, kt]) + _dot(s_im, cim_ref[0, kt])
            for i in range(lpk):
                yb_ref[kt * lpk + i] = yk[:, i * ln:(i + 1) * ln]
        for b in range(nb):
            for cb in range(w // ln):
                y_ref[0, :, b * w + cb * ln:b * w + (cb + 1) * ln] = yb_ref[cb, pl.ds(b, tc, stride=nb), :]

    @pl.when(j < nc_c)
    def _():
        run(uc_ref, yc_ref)

    @pl.when(j >= nc_c)
    def _():
        run(ul_ref, yl_ref)


def _s5_scan(uc, ul, a_bar, bre, bim, cre, cim, nb):
    lc, wide = uc.shape
    seq = ul.shape[0]
    w = wide // nb
    r = S5_TC * nb
    nc_c, nc_l = lc // S5_TC, seq // S5_TC
    kt_n = bre.shape[1]
    half = a_bar.shape[2]

    def c_idx(d, j):
        jc = jnp.minimum(j, nc_c - 1)
        return jnp.where(d == 0, jc, nc_c - 1 - jc)

    def l_idx(d, j):
        jl = jnp.maximum(j - nc_c, 0)
        return jnp.where(d == 0, jl, nc_l - 1 - jl)

    wspec = lambda shp: pl.BlockSpec((1,) + shp, lambda d, j: (d,) + (0,) * len(shp))
    return pl.pallas_call(
        functools.partial(_s5_kernel, nc_c=nc_c, tc=S5_TC, nb=nb, kt_n=kt_n, half=half),
        grid=(2, nc_c + nc_l),
        in_specs=[pl.BlockSpec((S5_TC, wide), lambda d, j: (c_idx(d, j), 0)),
                  pl.BlockSpec((S5_TC, wide), lambda d, j: (l_idx(d, j), 0)),
                  wspec(a_bar.shape[1:]), wspec(bre.shape[1:]), wspec(bim.shape[1:]),
                  wspec(cre.shape[1:]), wspec(cim.shape[1:])],
        out_specs=[pl.BlockSpec((1, S5_TC, wide), lambda d, j: (d, c_idx(d, j), 0)),
                   pl.BlockSpec((1, S5_TC, wide), lambda d, j: (d, l_idx(d, j), 0))],
        out_shape=[jax.ShapeDtypeStruct((2, lc, wide), F32), jax.ShapeDtypeStruct((2, seq, wide), F32)],
        scratch_shapes=[pltpu.VMEM((r, 2 * half), F32), pltpu.VMEM((nb, 2 * half), F32),
                        pltpu.VMEM((w // V7X_LANES, r, V7X_LANES), F32),
                        pltpu.VMEM((w // V7X_LANES, r, V7X_LANES), F32)],
        compiler_params=_cp(("arbitrary", "arbitrary")),
        name="s5_scan",
    )(uc, ul, a_bar, bre, bim, cre, cim)


def _s5_params(a_re, a_im, log_dt, b_re, b_im, c_re, c_im):
    l_re, l_im = a_re.astype(F32), a_im.astype(F32)
    dt = jnp.exp(log_dt.astype(F32))[..., None]
    mag = jnp.exp(l_re * dt)
    ab_re, ab_im = mag * jnp.cos(l_im * dt), mag * jnp.sin(l_im * dt)
    inv = 1.0 / (l_re * l_re + l_im * l_im)
    q_re = ((ab_re - 1.0) * l_re + ab_im * l_im) * inv
    q_im = (ab_im * l_re - (ab_re - 1.0) * l_im) * inv
    bb_re = q_re[..., None] * b_re.astype(F32) - q_im[..., None] * b_im.astype(F32)
    bb_im = q_re[..., None] * b_im.astype(F32) + q_im[..., None] * b_re.astype(F32)
    groups, p_dim, h_dim = bb_re.shape[1:]
    gl = V7X_LANES * 2 // h_dim
    kt = groups // gl
    eye = jnp.eye(gl, dtype=F32)
    a_pack = jnp.stack([ab_re.reshape(2, groups * p_dim), ab_im.reshape(2, groups * p_dim)], axis=1)

    def pack_b(x):
        x = x.reshape(2, kt, gl, p_dim, h_dim)
        return jnp.einsum('lm,dklph->dklhmp', eye, x).reshape(2, kt, gl * h_dim, gl * p_dim).astype(BF16)

    def pack_c(x):
        x = x.reshape(2, kt, gl, h_dim, p_dim)
        return jnp.einsum('lm,dklhp->dklpmh', eye, x).reshape(2, kt, gl * p_dim, gl * h_dim).astype(BF16)

    return (a_pack, pack_b(bb_re), pack_b(bb_im),
            pack_c(c_re.astype(F32)), pack_c(-c_im.astype(F32)))


def _s5_glu_kernel(u_ref, y_ref, d_ref, w_ref, b_ref, o_ref):
    y = d_ref[...] * u_ref[...] + y_ref[0] + y_ref[1]
    g = 0.5 * y * (1.0 + jnp.tanh(math.sqrt(2.0 / math.pi) * (y + 0.044715 * (y * y * y))))
    o_ref[...] = (g * _sigmoid(_dot(g.astype(BF16), w_ref[...]) + b_ref[...])).astype(BF16)


def _s5_glu(u_tm, y_tm, d_skip, glu_w, glu_b, seq, nb, tm):
    w = glu_w.shape[0]
    tps = seq // tm
    return pl.pallas_call(
        _s5_glu_kernel,
        grid=(nb, tps),
        in_specs=[pl.BlockSpec((tm, w), lambda b, i: (i, b)),
                  pl.BlockSpec((2, tm, w), lambda b, i: (0, i, b)),
                  pl.BlockSpec((1, w), lambda b, i: (0, 0)),
                  pl.BlockSpec((w, w), lambda b, i: (0, 0)),
                  pl.BlockSpec((1, w), lambda b, i: (0, 0))],
        out_specs=pl.BlockSpec((tm, w), lambda b, i: (b * tps + i, 0)),
        out_shape=jax.ShapeDtypeStruct((nb * seq, w), BF16),
        compiler_params=_cp(("parallel", "arbitrary")),
        name="s5_glu",
    )(u_tm, y_tm, d_skip.reshape(1, w), glu_w, glu_b.reshape(1, w))


def _na_groups(rows):
    wr = NA_GQ + NA_KH - 1
    plan, sigs = [], {}
    for g in range(rows // NA_GQ):
        r0 = g * NA_GQ
        ws = min(max(r0 - NA_KH // 2, 0), rows - wr)
        sig = tuple(min(max(r - NA_KH // 2, 0), rows - NA_KH) - ws for r in range(r0, r0 + NA_GQ)) + (r0 - ws,)
        tid = sigs.setdefault(sig, len(sigs))
        plan.append((r0, ws, tid))
    reps = [next(p for p in plan if p[2] == t) for t in range(len(sigs))]
    return plan, reps, wr


def _na_bias(rpb, rows):
    _, reps, wr = _na_groups(rows)
    col = np.arange(GRID_W)
    cs = np.clip(col - NA_KW // 2, 0, GRID_W - NA_KW)
    col_ok = (col[None, :] >= cs[:, None]) & (col[None, :] < cs[:, None] + NA_KW)
    dc = np.clip(col[None, :] - col[:, None] + NA_KW - 1, 0, 2 * NA_KW - 2)
    pick = (dc[None] == np.arange(2 * NA_KW - 1)[:, None, None]).astype(np.float32)
    tcol = jnp.einsum('hdc,cqk->hdqk', rpb.astype(F32), pick, precision=lax.Precision.HIGHEST)
    tcol = jnp.where(col_ok[None, None], tcol, NEG)
    dead = jnp.full(tcol[:, 0].shape, NEG, F32)
    tabs = []
    for r0, ws, _ in reps:
        row_blocks = []
        for r in range(r0, r0 + NA_GQ):
            rs = min(max(r - NA_KH // 2, 0), rows - NA_KH)
            blocks = [tcol[:, kr - r + NA_KH - 1] if rs <= kr < rs + NA_KH else dead for kr in range(ws, ws + wr)]
            row_blocks.append(jnp.concatenate(blocks, axis=-1))
        tabs.append(jnp.concatenate(row_blocks, axis=-2))
    return jnp.stack(tabs)


def _na_kernel(q_ref, k_ref, v_ref, qc_ref, kc_ref, vc_ref, bias_ref, o_ref, oc_ref, *, plan, wr, scale):
    kc = kc_ref[0]
    vc = vc_ref[0]
    nq = NA_GQ * GRID_W
    nk = wr * GRID_W
    for r0, ws, tid in plan:
        q = q_ref[0, r0 * GRID_W:r0 * GRID_W + nq, :]
        kw = k_ref[0, ws * GRID_W:ws * GRID_W + nk, :]
        vw = v_ref[0, ws * GRID_W:ws * GRID_W + nk, :]
        s1 = _dot_nt(q, kw) * scale + bias_ref[tid, 0]
        s2 = _dot_nt(q, kc) * scale
        m = jnp.maximum(jnp.max(s1, axis=-1, keepdims=True), jnp.max(s2, axis=-1, keepdims=True))
        p1 = jnp.exp(s1 - m)
        p2 = jnp.exp(s2 - m)
        den = jnp.sum(p1, axis=-1, keepdims=True) + jnp.sum(p2, axis=-1, keepdims=True)
        o = (_dot(p1.astype(BF16), vw) + _dot(p2.astype(BF16), vc)) / den
        o_ref[0, r0 * GRID_W:r0 * GRID_W + nq, :] = o.astype(BF16)
    s = _dot_nt(qc_ref[0], kc) * scale
    p = jnp.exp(s - jnp.max(s, axis=-1, keepdims=True))
    oc = _dot(p.astype(BF16), vc) / jnp.sum(p, axis=-1, keepdims=True)
    oc_ref[0] = oc.astype(BF16)


def _na_attention(qkv_l, qkv_c, bias, heads):
    bsz, seq, _ = qkv_l.shape
    lc = qkv_c.shape[1]
    hd = NA_HEAD_DIM
    plan, _, wr = _na_groups(seq // GRID_W)
    n_types = bias.shape[0]
    spec = lambda t, off: pl.BlockSpec((1, t, hd), lambda b, h: (b, 0, off + h))
    return pl.pallas_call(
        functools.partial(_na_kernel, plan=plan, wr=wr, scale=hd ** -0.5),
        grid=(bsz, heads),
        in_specs=[spec(seq, 0), spec(seq, heads), spec(seq, 2 * heads),
                  spec(lc, 0), spec(lc, heads), spec(lc, 2 * heads),
                  pl.BlockSpec((n_types, 1) + bias.shape[2:], lambda b, h: (0, h, 0, 0))],
        out_specs=[pl.BlockSpec((1, seq, hd), lambda b, h: (b, 0, h)),
                   pl.BlockSpec((1, lc, hd), lambda b, h: (b, 0, h))],
        out_shape=[jax.ShapeDtypeStruct((bsz, seq, heads * hd), BF16),
                   jax.ShapeDtypeStruct((bsz, lc, heads * hd), BF16)],
        compiler_params=_cp(("parallel", "arbitrary")),
        name="nbr_attention",
    )(qkv_l, qkv_l, qkv_l, qkv_c, qkv_c, qkv_c, bias)


def _dwconv3(load, t_len, r0, rows, w, b):
    main = load(r0, rows)
    idx = lax.broadcasted_iota(jnp.int32, main.shape, 0)
    prev = load(r0 - 1, 1) if r0 > 0 else jnp.zeros_like(main[0:1])
    nxt = load(r0 + rows, 1) if r0 + rows < t_len else jnp.zeros_like(main[0:1])
    up = jnp.where(idx == 0, prev, pltpu.roll(main, 1, 0))
    down = jnp.where(idx == rows - 1, nxt, pltpu.roll(main, rows - 1, 0))
    return w[0:1] * up + w[1:2] * main + w[2:3] * down + b


CONV_ROWS = 256


def _ssd_kernel(xc_ref, bc_ref, cc_ref, dtc_ref, xl_ref, bl_ref, cl_ref, dtl_ref,
                wx_ref, wb_ref, wc_ref, bx_ref, bb_ref, bcb_ref, dtb_ref, a_ref, dsk_ref,
                y_ref, xs_ref, bs_ref, cs_ref, dts_ref, cum_ref, st_ref, dec_ref, *, lc, seq, e_n):
    q = SSD_CHUNK
    p = SSD_HEAD_DIM
    ncc, ncl = lc // q, seq // q
    nch = ncc + ncl

    for src, dst, w_ref, b_ref in ((0, xs_ref, wx_ref, bx_ref), (1, bs_ref, wb_ref, bb_ref), (2, cs_ref, wc_ref, bcb_ref)):
        w = w_ref[...]
        b = b_ref[...]
        for base, t_len, ref in ((0, lc, (xc_ref, bc_ref, cc_ref)[src]), (lc, seq, (xl_ref, bl_ref, cl_ref)[src])):
            cr = min(CONV_ROWS, t_len)
            for r0 in range(0, t_len, cr):
                dst[base + r0:base + r0 + cr, :] = _silu(
                    _dwconv3(lambda a, n, ref=ref: ref[0, a:a + n, :], t_len, r0, cr, w, b))
    for base, t_len, ref in ((0, lc, dtc_ref), (lc, seq, dtl_ref)):
        v = ref[0] + dtb_ref[...]
        dts_ref[base:base + t_len, :] = jnp.maximum(v, 0.0) + jnp.log(1.0 + jnp.exp(-jnp.abs(v)))

    ii = lax.broadcasted_iota(jnp.int32, (q, q), 0)
    jj = lax.broadcasted_iota(jnp.int32, (q, q), 1)
    keep = (jj <= ii, jj >= ii)
    tri = tuple(jnp.where(m, 1.0, 0.0).astype(BF16) for m in keep)
    a_row = a_ref[...]
    dsk = dsk_ref[...]
    xw = e_n * p
    fwd_lane = lax.broadcasted_iota(jnp.int32, (q, V7X_LANES), 1) < e_n
    spread = jnp.where(lax.broadcasted_iota(jnp.int32, (V7X_LANES, 2 * xw), 0)
                       == lax.broadcasted_iota(jnp.int32, (V7X_LANES, 2 * xw), 1) // p, 1.0, 0.0).astype(BF16)

    def expand(v):
        hi, lo = _split2(v)
        return _dot(hi, spread) + _dot(lo, spread)

    def local(c, carry):
        r0 = pl.multiple_of(c * q, q)
        x = xs_ref[pl.ds(r0, q), :]
        dt = dts_ref[pl.ds(r0, q), :]
        dta = dt * a_row
        d_hi = dta.astype(BF16)
        r1 = dta - d_hi.astype(F32)
        d_mid = r1.astype(BF16)
        d_lo = (r1 - d_mid.astype(F32)).astype(BF16)
        bm_t = bs_ref[pl.ds(r0, q), :].T.astype(BF16)
        cum = [_dot(tri[k], d_hi) + (_dot(tri[k], d_mid) + _dot(tri[k], d_lo)) for k in range(2)]
        cum = jnp.where(fwd_lane, cum[0], cum[1])
        cum_ref[pl.ds(r0, q), :] = cum
        tot = jnp.where(fwd_lane[0:1], cum[q - 1:q, :], cum[0:1, :])
        wgt = expand(dt * jnp.exp(tot - cum))
        st = _dot(bm_t, (jnp.concatenate([x, x], axis=1) * wgt).astype(BF16))
        dec = expand(jnp.broadcast_to(jnp.exp(tot), (V7X_SUBLANES, V7X_LANES)))[0:1]
        for k in range(2):
            st_ref[k, c] = st[:, k * xw:(k + 1) * xw]
            dec_ref[k, c] = dec[:, k * xw:(k + 1) * xw]
        return carry

    lax.fori_loop(0, nch, local, 0, unroll=2)

    for k in range(2):
        def prop(i, s, k=k):
            c = i if k == 0 else jnp.where(i < ncc, ncc - 1 - i, nch - 1 - (i - ncc))
            loc = st_ref[k, c]
            st_ref[k, c] = s
            return s * dec_ref[k, c] + loc

        lax.fori_loop(0, nch, prop, jnp.zeros(st_ref.shape[2:], F32))

    def emit(i, carry):
        c = ncc + i
        r0 = pl.multiple_of(c * q, q)
        x = xs_ref[pl.ds(r0, q), :]
        dt = dts_ref[pl.ds(r0, q), :]
        cm = cs_ref[pl.ds(r0, q), :].astype(BF16)
        gmat = _dot_nt(cm, bs_ref[pl.ds(r0, q), :].astype(BF16))
        cum = cum_ref[pl.ds(r0, q), :]
        cum_t = cum.T
        xdt = (jnp.concatenate([x, x], axis=1) * expand(dt)).astype(BF16)
        ydiag = []
        for e in range(e_n):
            lhs, rhs = [], []
            for k in range(2):
                col = k * e_n + e
                seg = jnp.where(keep[k], cum[:, col:col + 1] - cum_t[col:col + 1, :], NEG)
                lhs.append((gmat * jnp.exp(seg)).astype(BF16))
                rhs.append(xdt[:, k * xw + e * p:k * xw + (e + 1) * p])
            ydiag.append(_dot(jnp.concatenate(lhs, axis=1), jnp.concatenate(rhs, axis=0)))
        y = jnp.concatenate(ydiag, axis=1) + dsk * x
        eoff = expand(jnp.exp(cum))
        for k in range(2):
            y = y + _dot(cm, st_ref[k, c].astype(BF16)) * eoff[:, k * xw:(k + 1) * xw]
        y_ref[0, pl.ds(pl.multiple_of(i * q, q), q), :] = y
        return carry

    lax.fori_loop(0, ncl, emit, 0, unroll=2)


def _ssd(p_l, dt_l, p_c, dt_c, x_off_l, x_off_c, conv_w, conv_b, dtb, a_neg, d_exp, inner):
    bsz, seq, _ = p_l.shape
    lc = p_c.shape[1]
    e_n = inner // SSD_HEAD_DIM // SSD_GROUPS
    xw = e_n * SSD_HEAD_DIM
    n = SSD_STATE
    t_tot = lc + seq

    def specs(t, off):
        return [pl.BlockSpec((1, t, xw), lambda b, g: (b, 0, off // xw + g)),
                pl.BlockSpec((1, t, n), lambda b, g: (b, 0, (off + inner) // n + g)),
                pl.BlockSpec((1, t, n), lambda b, g: (b, 0, (off + inner) // n + SSD_GROUPS + g)),
                pl.BlockSpec((1, t, V7X_LANES), lambda b, g: (b, 0, g))]

    def pspecs(rows):
        return [pl.BlockSpec((rows, xw), lambda b, g: (0, g)),
                pl.BlockSpec((rows, n), lambda b, g: (0, inner // n + g)),
                pl.BlockSpec((rows, n), lambda b, g: (0, inner // n + SSD_GROUPS + g))]

    lane = lambda: pl.BlockSpec((1, V7X_LANES), lambda b, g: (0, g))
    return pl.pallas_call(
        functools.partial(_ssd_kernel, lc=lc, seq=seq, e_n=e_n),
        grid=(bsz, SSD_GROUPS),
        in_specs=specs(lc, x_off_c) + specs(seq, x_off_l) + pspecs(3) + pspecs(1)
        + [lane(), lane(), pl.BlockSpec((1, xw), lambda b, g: (0, g))],
        out_specs=pl.BlockSpec((1, seq, xw), lambda b, g: (b, 0, g)),
        out_shape=jax.ShapeDtypeStruct((bsz, seq, inner), F32),
        scratch_shapes=[pltpu.VMEM((t_tot, xw), F32), pltpu.VMEM((t_tot, n), F32), pltpu.VMEM((t_tot, n), F32),
                        pltpu.VMEM((t_tot, V7X_LANES), F32), pltpu.VMEM((t_tot, V7X_LANES), F32),
                        pltpu.VMEM((2, t_tot // SSD_CHUNK, n, xw), F32), pltpu.VMEM((2, t_tot // SSD_CHUNK, 1, xw), F32)],
        compiler_params=_cp(("parallel", "arbitrary")),
        name="ssd",
    )(p_c, p_c, p_c, dt_c, p_l, p_l, p_l, dt_l, conv_w, conv_w, conv_w,
      conv_b, conv_b, conv_b, dtb, a_neg, d_exp)


def _hy_filter_kernel(z_ref, w_in_ref, b_in_ref, w_mid_ref, b_mid_ref, w_out_ref, freq_ref, delta_ref, o_ref, h_ref):
    j = pl.program_id(1)
    hp = lax.Precision.HIGHEST

    @pl.when(j == 0)
    def _():
        freq = freq_ref[...]
        h = jnp.sin(freq * (jnp.dot(z_ref[...], w_in_ref[...], precision=hp, preferred_element_type=F32) + b_in_ref[...]))
        for i in range(w_mid_ref.shape[0]):
            h = jnp.sin(freq * (jnp.dot(h, w_mid_ref[i], precision=hp, preferred_element_type=F32) + b_mid_ref[i]))
        h_ref[...] = h

    decay = jnp.exp(-z_ref[:, 0:1] * delta_ref[...])
    o_ref[...] = jnp.dot(h_ref[...], w_out_ref[...], precision=hp, preferred_element_type=F32) * decay


def _hy_filters(seq, w_in, b_in, w_mid, b_mid, w_out, freq, width):
    t = jnp.linspace(0.0, 1.0, seq, dtype=F32)[:, None]
    w = 2.0 * math.pi * jnp.arange(seq, dtype=F32)[:, None] / seq
    f = jnp.linspace(1e-4, HY_BANDS - 1, HY_BANDS, dtype=F32)[None, :]
    z = jnp.concatenate([t, jnp.cos(f * w), -jnp.sin(f * w)], axis=-1)
    emb, hid = w_in.shape
    pad = V7X_LANES
    z = jnp.pad(z, ((0, 0), (0, pad - emb)))
    w_in_p = jnp.pad(w_in.astype(F32), ((0, pad - emb), (0, pad - hid)))
    padv = lambda v: jnp.pad(v.astype(F32), ((0, 0), (0, pad - hid)))
    w_mid_p = jnp.pad(w_mid.astype(F32), ((0, 0), (0, pad - hid), (0, pad - hid)))
    w_out_p = jnp.pad(w_out.astype(F32), ((0, pad - hid), (0, 0)))
    n_out = w_out.shape[1]
    max_decay = math.log(HY_DECAY_TARGET) / HY_FAST_PCT
    min_decay = math.log(HY_DECAY_TARGET) / HY_SLOW_PCT
    deltas = jnp.abs(jnp.linspace(min_decay, max_decay, width, dtype=F32))[None, :]
    tr = 256
    n_mid = w_mid.shape[0]
    return pl.pallas_call(
        _hy_filter_kernel,
        grid=(seq // tr, n_out // width),
        in_specs=[pl.BlockSpec((tr, pad), lambda i, j: (i, 0)),
                  pl.BlockSpec((pad, pad), lambda i, j: (0, 0)),
                  pl.BlockSpec((1, pad), lambda i, j: (0, 0)),
                  pl.BlockSpec((n_mid, pad, pad), lambda i, j: (0, 0, 0)),
                  pl.BlockSpec((n_mid, 1, pad), lambda i, j: (0, 0, 0)),
                  pl.BlockSpec((pad, width), lambda i, j: (0, j)),
                  pl.BlockSpec((1, pad), lambda i, j: (0, 0)),
                  pl.BlockSpec((1, width), lambda i, j: (0, 0))],
        out_specs=pl.BlockSpec((tr, width), lambda i, j: (i, j)),
        out_shape=jax.ShapeDtypeStruct((seq, n_out), F32),
        scratch_shapes=[pltpu.VMEM((tr, pad), F32)],
        compiler_params=_cp(("parallel", "arbitrary")),
        name="hyena_filter",
    )(z, w_in_p, padv(b_in[None]), w_mid_p, padv(b_mid)[:, None, :], w_out_p, padv(freq[None]), deltas)


def _dft_tables(seq):
    n = 2 * seq
    k = jnp.arange(seq, dtype=jnp.int32)
    ang = (2.0 * math.pi / n) * ((k[:, None] * k[None, :]) % n).astype(F32)
    cos = jnp.cos(ang)
    sin = jnp.sin(ang)
    alt = jnp.where(k % 2 == 0, 1.0, -1.0).astype(F32)
    sf = jnp.where(k[:, None] == 0, alt[None, :], sin)
    wk = jnp.where(k == 0, 1.0, 2.0).astype(F32) / n
    ci = cos * wk[None, :]
    si = jnp.where(k[None, :] == 0, alt[:, None] / n, sin * (2.0 / n))
    return _split2(cos) + _split2(sf) + (ci.astype(BF16), si.astype(BF16))


def _hy_spec_kernel(hf_ref, hb_ref, cfh_ref, cfl_ref, sfh_ref, sfl_ref, kr_ref, ki_ref,
                    smh_ref, sml_ref, dfh_ref, dfl_ref, nyq_ref):
    k = pl.program_id(1)

    @pl.when(k == 0)
    def _():
        hf = hf_ref[...]
        t = lax.broadcasted_iota(jnp.int32, hf.shape, 0)
        hb = jnp.where(t == 0, 0.0, hb_ref[...])
        sm = hf + hb
        df = hf - hb
        smh_ref[...], sml_ref[...] = _split2(sm)
        dfh_ref[...], dfl_ref[...] = _split2(df)
        nyq_ref[...] = jnp.sum(jnp.where(t % 2 == 0, sm, -sm), axis=0, keepdims=True)

    kr_ref[...] = _dot3(cfh_ref[...], cfl_ref[...], smh_ref[...], sml_ref[...])
    ki = _dot3(sfh_ref[...], sfl_ref[...], dfh_ref[...], dfl_ref[...])
    row = lax.broadcasted_iota(jnp.int32, ki.shape, 0)
    ki_ref[...] = jnp.where((row == 0) & (k == 0), nyq_ref[...], ki)


def _hy_spectra(h, tabs, width):
    seq, n_cols = h.shape
    order = n_cols // (2 * width)
    cpb = width // HY_TC
    cfh, cfl, sfh, sfl = tabs[:4]
    fspec = lambda: pl.BlockSpec((HY_TK, seq), lambda c, k: (k, 0))
    return pl.pallas_call(
        _hy_spec_kernel,
        grid=(order * cpb, seq // HY_TK),
        in_specs=[pl.BlockSpec((seq, HY_TC), lambda c, k: (0, (c // cpb) * 2 * cpb + c % cpb)),
                  pl.BlockSpec((seq, HY_TC), lambda c, k: (0, (c // cpb) * 2 * cpb + cpb + c % cpb)),
                  fspec(), fspec(), fspec(), fspec()],
        out_specs=[pl.BlockSpec((HY_TK, HY_TC), lambda c, k: (k, c)),
                   pl.BlockSpec((HY_TK, HY_TC), lambda c, k: (k, c))],
        out_shape=[jax.ShapeDtypeStruct((seq, order * width), F32)] * 2,
        scratch_shapes=[pltpu.VMEM((seq, HY_TC), BF16)] * 4 + [pltpu.VMEM((1, HY_TC), F32)],
        compiler_params=_cp(("parallel", "arbitrary")),
        name="hyena_spectra",
    )(h, h, cfh, cfl, sfh, sfl)


def _hy_conv_kernel(u_ref, g_ref, wu_ref, bu_ref, wg_ref, bg_ref, kr_ref, ki_ref, fb_ref,
                    cf_ref, sf_ref, ci_ref, si_ref,
                    o_ref, ub_ref, acc_ref, *, conv_u, seq):
    k = pl.program_id(2)
    cr = CONV_ROWS

    def u_rows(r0):
        if conv_u:
            return _dwconv3(lambda a, n: u_ref[0, a:a + n, :], seq, r0, cr, wu_ref[...], bu_ref[...])
        return u_ref[0, r0:r0 + cr, :]

    @pl.when(k == 0)
    def _():
        for r0 in range(0, seq, cr):
            ub_ref[r0:r0 + cr, :] = u_rows(r0).astype(BF16)
        acc_ref[...] = jnp.zeros_like(acc_ref)

    ub = ub_ref[...]
    a = _dot(cf_ref[...], ub)
    b = _dot(sf_ref[...], ub)
    kr = kr_ref[...]
    ki = ki_ref[...]
    first = (lax.broadcasted_iota(jnp.int32, a.shape, 0) == 0) & (k == 0)
    bki = b * ki
    yc = a * kr - jnp.where(first, 0.0, bki)
    ys = jnp.where(first, bki, a * ki + b * kr)
    acc_ref[...] += _dot(ci_ref[...], yc.astype(BF16)) + _dot(si_ref[...], ys.astype(BF16))

    @pl.when(k == pl.num_programs(2) - 1)
    def _():
        for r0 in range(0, seq, cr):
            gate = _dwconv3(lambda a_, n: g_ref[0, a_:a_ + n, :], seq, r0, cr, wg_ref[...], bg_ref[...])
            o_ref[0, r0:r0 + cr, :] = gate * (acc_ref[r0:r0 + cr, :] + u_rows(r0) * fb_ref[0])


def _hy_conv(u, u_blk, g, g_blk, short_w, short_b, wu_blk, wg_blk, kr, ki, k_blk, fbias, order, tabs, width, conv_u):
    bsz, seq, _ = g.shape
    cpb = width // HY_TC
    fspec = lambda: pl.BlockSpec((HY_TK, seq), lambda b, c, k: (k, 0))
    ispec = lambda: pl.BlockSpec((seq, HY_TK), lambda b, c, k: (0, k))
    once = dict(pipeline_mode=pl.Buffered(1))
    return pl.pallas_call(
        functools.partial(_hy_conv_kernel, conv_u=conv_u, seq=seq),
        grid=(bsz, cpb, seq // HY_TK),
        in_specs=[pl.BlockSpec((1, seq, HY_TC), lambda b, c, k: (b, 0, u_blk + c), **once),
                  pl.BlockSpec((1, seq, HY_TC), lambda b, c, k: (b, 0, g_blk + c), **once),
                  pl.BlockSpec((3, HY_TC), lambda b, c, k: (0, wu_blk + c)),
                  pl.BlockSpec((1, HY_TC), lambda b, c, k: (0, wu_blk + c)),
                  pl.BlockSpec((3, HY_TC), lambda b, c, k: (0, wg_blk + c)),
                  pl.BlockSpec((1, HY_TC), lambda b, c, k: (0, wg_blk + c)),
                  pl.BlockSpec((HY_TK, HY_TC), lambda b, c, k: (k, k_blk + c)),
                  pl.BlockSpec((HY_TK, HY_TC), lambda b, c, k: (k, k_blk + c)),
                  pl.BlockSpec((1, 1, HY_TC), lambda b, c, k: (order, 0, c)),
                  fspec(), fspec(), ispec(), ispec()],
        out_specs=pl.BlockSpec((1, seq, HY_TC), lambda b, c, k: (b, 0, c)),
        out_shape=jax.ShapeDtypeStruct((bsz, seq, width), F32),
        scratch_shapes=[pltpu.VMEM((seq, HY_TC), BF16), pltpu.VMEM((seq, HY_TC), F32)],
        compiler_params=_cp(("parallel", "parallel", "arbitrary")),
        name="hyena_conv",
    )(u, g, short_w, short_b, short_w, short_b, kr, ki, fbias, tabs[0], tabs[2], tabs[4], tabs[5])


def kernel(x, c, ctx, c_ctx, mod_w, mod_b, norm_g, ffn_wg, ffn_wu, ffn_wd, final_g,
           ev_w_in, ev_w_out, s5_a_re, s5_a_im, s5_log_dt, s5_b_re, s5_b_im, s5_c_re, s5_c_im,
           s5_d, s5_glu_w, s5_glu_b, na_rpb,
           od_w_in, od_w_out, hy_short_w, hy_short_b, hy_w_in, hy_b_in, hy_w_mid, hy_b_mid,
           hy_w_out, hy_freq, hy_fbias,
           ssd_conv_w, ssd_conv_b, ssd_dt_bias, ssd_a_log, ssd_d, ssd_norm_g):
    bsz, seq, d = x.shape
    lc = ctx.shape[1]
    depth = mod_w.shape[0]
    assert depth == 2 and bsz == V7X_SUBLANES, "layer 0 = S5 || attention with context output, layer 1 = Hyena || SSD"
    assert seq % TM == 0 and (bsz * lc) % TM == 0 and lc % CONV_ROWS == 0

    xl = x.reshape(bsz * seq, d)
    xc = ctx.reshape(bsz * lc, d)
    tps = seq // TM
    mod_l = lambda i: i // tps
    mod_c = lambda i: bsz

    cond = jnp.concatenate([c, c_ctx[None], jnp.zeros((V7X_SUBLANES - 1, d), F32)], axis=0)
    mods = _modulation(cond, mod_w, mod_b).reshape(depth, cond.shape[0], N_MOD, d)

    ffn_w = {(layer, half): tuple(w[layer, half].astype(BF16) for w in (ffn_wg, ffn_wu, ffn_wd))
             for layer in range(depth) for half in range(2)}

    def ffn(xx, layer, half, mod_of, final=False):
        return _ffn(xx, mods[layer], mod_of, norm_g[layer, 2 * half], *ffn_w[layer, half],
                    final_g, 2 * half, final, TM)

    xl = ffn(xl, 0, 0, mod_l)
    xc = ffn(xc, 0, 0, mod_c)
    w_in = ev_w_in[0].astype(BF16)
    u_l, qkv_l = _inproj_even(xl, mods[0], mod_l, norm_g[0, 1], w_in, seq, bsz, TM)
    u_c, qkv_c = _inproj_even(xc, mods[0], mod_c, norm_g[0, 1], w_in, lc, bsz, lc)
    s5p = _s5_params(s5_a_re[0], s5_a_im[0], s5_log_dt[0], s5_b_re[0], s5_b_im[0], s5_c_re[0], s5_c_im[0])
    y_c, y_l = _s5_scan(u_c, u_l, *s5p, bsz)
    glu_w = s5_glu_w[0].astype(BF16)
    a_l = _s5_glu(u_l, y_l, s5_d[0], glu_w, s5_glu_b[0], seq, bsz, TM)
    a_c = _s5_glu(u_c, y_c, s5_d[0], glu_w, s5_glu_b[0], lc, bsz, lc)
    heads = na_rpb.shape[1]
    bias = _na_bias(na_rpb[0], seq // GRID_W)
    o_l, o_c = _na_attention(qkv_l.reshape(bsz, seq, -1), qkv_c.reshape(bsz, lc, -1), bias, heads)
    w_out = ev_w_out[0].astype(BF16)
    xl = _outproj_even(a_l, o_l.reshape(bsz * seq, -1), xl, mods[0], mod_l, w_out, TM)
    xc = _outproj_even(a_c, o_c.reshape(bsz * lc, -1), xc, mods[0], mod_c, w_out, TM)
    xl = ffn(xl, 0, 1, mod_l)
    xc = ffn(xc, 0, 1, mod_c)

    xl = ffn(xl, 1, 0, mod_l)
    xc = ffn(xc, 1, 0, mod_c)
    hyw = hy_fbias.shape[2]
    inner = ssd_norm_g.shape[1]
    n_heads = ssd_d.shape[1]
    e_n = n_heads // SSD_GROUPS
    o_z = 3 * hyw
    o_xbc = o_z + inner
    o_dt = o_xbc + inner + 2 * SSD_GROUPS * SSD_STATE
    w_od = od_w_in[0]
    w_dt = w_od[:, o_dt:].reshape(d, 2, SSD_GROUPS, e_n)
    w_dt = jnp.transpose(w_dt, (0, 2, 1, 3)).reshape(d, SSD_GROUPS, 2 * e_n)
    w_dt = jnp.pad(w_dt, ((0, 0), (0, 0), (0, V7X_LANES - 2 * e_n))).reshape(d, SSD_GROUPS * V7X_LANES).astype(BF16)

    def regroup(v):
        v = jnp.transpose(v.astype(F32).reshape(2, SSD_GROUPS, e_n), (1, 0, 2)).reshape(SSD_GROUPS, 2 * e_n)
        return jnp.pad(v, ((0, 0), (0, V7X_LANES - 2 * e_n))).reshape(1, SSD_GROUPS * V7X_LANES)

    w_main = w_od[:, :o_dt].astype(BF16)
    p_l, dt_l = _inproj_odd(xl, mods[1], mod_l, norm_g[1, 1], w_main, w_dt, TM)
    p_c, dt_c = _inproj_odd(xc, mods[1], mod_c, norm_g[1, 1], w_main[:, o_xbc:], w_dt, TM)
    p_l3 = p_l.reshape(bsz, seq, -1)
    y_ssd = _ssd(p_l3, dt_l.reshape(bsz, seq, -1), p_c.reshape(bsz, lc, -1), dt_c.reshape(bsz, lc, -1),
                 o_xbc, 0, ssd_conv_w[0], ssd_conv_b[0][None], regroup(ssd_dt_bias[0]),
                 regroup(-jnp.exp(ssd_a_log[0].astype(F32))),
                 jnp.repeat(ssd_d[0].astype(F32), SSD_HEAD_DIM)[None], inner)

    filt = _hy_filters(seq, hy_w_in[0], hy_b_in[0], hy_w_mid[0], hy_b_mid[0], hy_w_out[0], hy_freq[0], hyw)
    tabs = _dft_tables(seq)
    kr, ki = _hy_spectra(filt, tabs, hyw)
    cpb = hyw // HY_TC
    sw, sb = hy_short_w[0], hy_short_b[0][None]
    fb = hy_fbias[0][:, None, :]
    z1 = _hy_conv(p_l3, 2 * cpb, p_l3, 0, sw, sb, 2 * cpb, 0, kr, ki, 0, fb, 0, tabs, hyw, True)
    y_hy = _hy_conv(z1, 0, p_l3, cpb, sw, sb, 0, cpb, kr, ki, cpb, fb, 1, tabs, hyw, False)

    xl = _outproj_odd(y_hy.reshape(bsz * seq, hyw), y_ssd.reshape(bsz * seq, inner), p_l, o_z // inner,
                      ssd_norm_g[0], xl, mods[1], mod_l, od_w_out[0].astype(BF16), TM)
    xl = ffn(xl, 1, 1, mod_l, final=True)
    return xl.reshape(bsz, seq, d)
```

```python
import functools
import math

import numpy as np
import jax
import jax.numpy as jnp
from jax import lax
from jax.experimental import pallas as pl
from jax.experimental.pallas import tpu as pltpu

F32 = jnp.float32
BF16 = jnp.bfloat16

EPS = 1e-6
GRID_W = 64
N_MOD = 9
S5_GROUP = 16
S5_STATE = 64
NA_HEAD_DIM = 128
NA_KH = 8
NA_KW = 16
HY_BANDS = 16
HY_DECAY_TARGET = 1e-2
HY_FAST_PCT = 0.3
HY_SLOW_PCT = 1.5
SSD_HEAD_DIM = 64
SSD_GROUPS = 4
SSD_STATE = 128
SSD_CHUNK = 128

V7X_LANES = 128
V7X_SUBLANES = 8
V7X_VMEM_BYTES = 64 * 1024 * 1024
VMEM_LIMIT = V7X_VMEM_BYTES - 12 * 1024 * 1024

TM = 512
TM_WIDE = 1024
ROW_BLOCK = 16
TF = 512
TN = 1024
S5_TC = 32
S5_W = 512
NA_GQ = 4
HY_TC = 512
HY_TK = 256
DFT_BLOCK = 64
NEG = -1e30


def _cp(sem, vmem=VMEM_LIMIT):
    return pltpu.CompilerParams(dimension_semantics=sem, vmem_limit_bytes=vmem)


def _sigmoid(x):
    return 1.0 / (1.0 + jnp.exp(-x))


def _silu(x):
    return x * _sigmoid(x)


def _dot(a, b):
    return jnp.dot(a, b, preferred_element_type=F32)


def _dot_nt(a, b):
    return lax.dot_general(a, b, (((1,), (1,)), ((), ())), preferred_element_type=F32)


def _split2(x):
    hi = x.astype(BF16)
    lo = (x - hi.astype(F32)).astype(BF16)
    return hi, lo


def _dot3(a_hi, a_lo, b_hi, b_lo):
    return _dot(a_hi, b_hi) + (_dot(a_hi, b_lo) + _dot(a_lo, b_hi))


def _rms(x):
    return x * lax.rsqrt(jnp.mean(x * x, axis=-1, keepdims=True) + EPS)


def _row_blocks(n_rows, body):
    def step(r, c):
        body(pl.ds(pl.multiple_of(r * ROW_BLOCK, ROW_BLOCK), ROW_BLOCK))
        return c

    lax.fori_loop(0, n_rows // ROW_BLOCK, step, 0, unroll=2)


def _rms_scale_into(row_fn, stats_ref, n_rows, width):
    def part(rows):
        sq = row_fn(rows)
        sq = sq * sq
        acc = sq[:, 0:V7X_LANES]
        for j in range(1, width // V7X_LANES):
            acc = acc + sq[:, j * V7X_LANES:(j + 1) * V7X_LANES]
        stats_ref[rows, :] = acc

    _row_blocks(n_rows, part)
    tot = jnp.sum(stats_ref[...], axis=-1, keepdims=True)
    stats_ref[...] = jnp.broadcast_to(lax.rsqrt(tot * (1.0 / width) + EPS), stats_ref.shape)


def _lanes(stat, width):
    return pltpu.repeat(stat, width // V7X_LANES, axis=1)


def _prenorm_into(x_ref, xn_ref, g_ref, mod_ref, sub, stats_ref, aff_ref):
    n_rows, width = x_ref.shape
    _rms_scale_into(lambda rows: x_ref[rows, :], stats_ref, n_rows, width)
    gain = g_ref[...] * (1.0 + mod_ref[0, 3 * sub + 1:3 * sub + 2, :])
    aff_ref[0] = jnp.broadcast_to(gain, (ROW_BLOCK, width))
    aff_ref[1] = jnp.broadcast_to(mod_ref[0, 3 * sub:3 * sub + 1, :], (ROW_BLOCK, width))

    def body(rows):
        y = x_ref[rows, :] * _lanes(stats_ref[rows, :], width) * aff_ref[0] + aff_ref[1]
        xn_ref[rows, :] = y.astype(BF16)

    _row_blocks(n_rows, body)


def _mod_kernel(c_ref, w_ref, b_ref, o_ref):
    cs = _silu(c_ref[...]).astype(BF16)
    o_ref[0] = _dot(cs, w_ref[0].astype(BF16)) + b_ref[0]


def _modulation(cond, mod_w, mod_b):
    depth, d, n = mod_w.shape
    rows = cond.shape[0]
    return pl.pallas_call(
        _mod_kernel,
        grid=(depth, n // TN),
        in_specs=[pl.BlockSpec((rows, d), lambda l, j: (0, 0)),
                  pl.BlockSpec((1, d, TN), lambda l, j: (l, 0, j)),
                  pl.BlockSpec((1, 1, TN), lambda l, j: (l, 0, j))],
        out_specs=pl.BlockSpec((1, rows, TN), lambda l, j: (l, 0, j)),
        out_shape=jax.ShapeDtypeStruct((depth, rows, n), F32),
        compiler_params=_cp(("arbitrary", "arbitrary")),
        name="adaln_mod",
    )(cond, mod_w, mod_b.reshape(depth, 1, n))


def _ffn_kernel(x_ref, mod_ref, g_ref, wg_ref, wu_ref, wd_ref, fg_ref, o_ref, xn_ref, acc_ref, stats_ref, aff_ref,
                *, sub, final):
    f = pl.program_id(1)
    n_rows, width = x_ref.shape

    @pl.when(f == 0)
    def _():
        _prenorm_into(x_ref, xn_ref, g_ref, mod_ref, sub, stats_ref, aff_ref)
        acc_ref[...] = jnp.zeros_like(acc_ref)

    xn = xn_ref[...]
    h = (_silu(_dot(xn, wg_ref[...])) * _dot(xn, wu_ref[...])).astype(BF16)
    acc_ref[...] += _dot(h, wd_ref[...])

    @pl.when(f == pl.num_programs(1) - 1)
    def _():
        def body(rows):
            o_ref[rows, :] = x_ref[rows, :] + 0.5 * mod_ref[0, 3 * sub + 2:3 * sub + 3, :] * acc_ref[rows, :]

        _row_blocks(n_rows, body)
        if final:
            _rms_scale_into(lambda rows: o_ref[rows, :], stats_ref, n_rows, width)

            def norm(rows):
                o_ref[rows, :] = o_ref[rows, :] * _lanes(stats_ref[rows, :], width) * fg_ref[...]

            _row_blocks(n_rows, norm)


def _ffn(x, mods, mod_of_tile, norm_g, wg, wu, wd, layer, half, final_g, final, tm):
    n_tok, d = x.shape
    ff = wg.shape[3]
    sub = 2 * half
    return pl.pallas_call(
        functools.partial(_ffn_kernel, sub=sub, final=final),
        grid=(n_tok // tm, ff // TF),
        in_specs=[pl.BlockSpec((tm, d), lambda i, f: (i, 0)),
                  pl.BlockSpec((1, N_MOD, d), lambda i, f: (mod_of_tile(i), 0, 0)),
                  pl.BlockSpec((1, d), lambda i, f: (0, 0)),
                  pl.BlockSpec((None, None, d, TF), lambda i, f: (layer, half, 0, f)),
                  pl.BlockSpec((None, None, d, TF), lambda i, f: (layer, half, 0, f)),
                  pl.BlockSpec((None, None, TF, d), lambda i, f: (layer, half, f, 0)),
                  pl.BlockSpec((1, d), lambda i, f: (0, 0))],
        out_specs=pl.BlockSpec((tm, d), lambda i, f: (i, 0)),
        out_shape=jax.ShapeDtypeStruct((n_tok, d), F32),
        scratch_shapes=[pltpu.VMEM((tm, d), BF16), pltpu.VMEM((tm, d), F32), pltpu.VMEM((tm, V7X_LANES), F32),
                        pltpu.VMEM((2, ROW_BLOCK, d), F32)],
        compiler_params=_cp(("parallel", "arbitrary")),
        name="swiglu",
    )(x, mods, norm_g.reshape(1, d), wg, wu, wd, final_g.reshape(1, d))


def _inproj_even_kernel(x_ref, mod_ref, g_ref, w_ref, u_ref, qkv_ref, xn_ref, stats_ref, aff_ref, *, s5w):
    _prenorm_into(x_ref, xn_ref, g_ref, mod_ref, 1, stats_ref, aff_ref)
    xn = xn_ref[...]
    u_ref[...] = _dot(xn, w_ref[:, :s5w])
    qkv_ref[...] = _dot(xn, w_ref[:, s5w:]).astype(BF16)


def _inproj_even(x, mods, mod_of_tile, norm_g, w, s5w, seq, nb, tm):
    n_tok, d = x.shape
    n_in = w.shape[1]
    tps = seq // tm
    return pl.pallas_call(
        functools.partial(_inproj_even_kernel, s5w=s5w),
        grid=(n_tok // tm,),
        in_specs=[pl.BlockSpec((tm, d), lambda i: (i, 0)),
                  pl.BlockSpec((1, N_MOD, d), lambda i: (mod_of_tile(i), 0, 0)),
                  pl.BlockSpec((1, d), lambda i: (0, 0)),
                  pl.BlockSpec((d, n_in), lambda i: (0, 0), pipeline_mode=pl.Buffered(1))],
        out_specs=[pl.BlockSpec((tm, s5w), lambda i: (i % tps, i // tps)),
                   pl.BlockSpec((tm, n_in - s5w), lambda i: (i, 0))],
        out_shape=[jax.ShapeDtypeStruct((seq, nb * s5w), F32),
                   jax.ShapeDtypeStruct((n_tok, n_in - s5w), BF16)],
        scratch_shapes=[pltpu.VMEM((tm, d), BF16), pltpu.VMEM((tm, V7X_LANES), F32), pltpu.VMEM((2, ROW_BLOCK, d), F32)],
        compiler_params=_cp(("parallel",)),
        name="inproj_even",
    )(x, mods, norm_g.reshape(1, d), w)


def _inproj_odd_kernel(x_ref, mod_ref, g_ref, w_ref, wdt_ref, p_ref, dt_ref, xn_ref, stats_ref, aff_ref):
    n = pl.program_id(1)

    @pl.when(n == 0)
    def _():
        _prenorm_into(x_ref, xn_ref, g_ref, mod_ref, 1, stats_ref, aff_ref)
        dt_ref[...] = _dot(xn_ref[...], wdt_ref[...])

    p_ref[...] = _dot(xn_ref[...], w_ref[...])


def _inproj_odd(x, mods, mod_of_tile, norm_g, w, wdt, tm):
    n_tok, d = x.shape
    n_in = w.shape[1]
    n_dt = wdt.shape[1]
    return pl.pallas_call(
        _inproj_odd_kernel,
        grid=(n_tok // tm, n_in // TN),
        in_specs=[pl.BlockSpec((tm, d), lambda i, n: (i, 0)),
                  pl.BlockSpec((1, N_MOD, d), lambda i, n: (mod_of_tile(i), 0, 0)),
                  pl.BlockSpec((1, d), lambda i, n: (0, 0)),
                  pl.BlockSpec((d, TN), lambda i, n: (0, n)),
                  pl.BlockSpec((d, n_dt), lambda i, n: (0, 0))],
        out_specs=[pl.BlockSpec((tm, TN), lambda i, n: (i, n)),
                   pl.BlockSpec((tm, n_dt), lambda i, n: (i, 0))],
        out_shape=[jax.ShapeDtypeStruct((n_tok, n_in), F32),
                   jax.ShapeDtypeStruct((n_tok, n_dt), F32)],
        scratch_shapes=[pltpu.VMEM((tm, d), BF16), pltpu.VMEM((tm, V7X_LANES), F32), pltpu.VMEM((2, ROW_BLOCK, d), F32)],
        compiler_params=_cp(("parallel", "arbitrary")),
        name="inproj_odd",
    )(x, mods, norm_g.reshape(1, d), w, wdt)


def _outproj_even_kernel(a_ref, b_ref, x_ref, mod_ref, wa_ref, wb_ref, o_ref):
    y = _dot(a_ref[...], wa_ref[...]) + _dot(b_ref[...], wb_ref[...])
    o_ref[...] = x_ref[...] + mod_ref[0, 5:6, :] * y


def _outproj_even(a, b, x, mods, mod_of_tile, w, tm):
    n_tok, d = x.shape
    half = a.shape[1]
    return pl.pallas_call(
        _outproj_even_kernel,
        grid=(n_tok // tm,),
        in_specs=[pl.BlockSpec((tm, half), lambda i: (i, 0)),
                  pl.BlockSpec((tm, half), lambda i: (i, 0)),
                  pl.BlockSpec((tm, d), lambda i: (i, 0)),
                  pl.BlockSpec((1, N_MOD, d), lambda i: (mod_of_tile(i), 0, 0)),
                  pl.BlockSpec((half, d), lambda i: (0, 0)),
                  pl.BlockSpec((half, d), lambda i: (1, 0))],
        out_specs=pl.BlockSpec((tm, d), lambda i: (i, 0)),
        out_shape=jax.ShapeDtypeStruct((n_tok, d), F32),
        compiler_params=_cp(("parallel",)),
        name="outproj_even",
    )(a, b, x, mods, w, w)


def _outproj_odd_kernel(hy_ref, ys_ref, z_ref, ng_ref, x_ref, mod_ref, wa_ref, wb_ref, o_ref, b_ref, y_ref, stats_ref):
    n_rows, width = ys_ref.shape

    def gated(rows):
        y_ref[rows, :] = ys_ref[rows, :] * _silu(z_ref[rows, :])

    _row_blocks(n_rows, gated)
    _rms_scale_into(lambda rows: y_ref[rows, :], stats_ref, n_rows, width)

    def body(rows):
        b_ref[rows, :] = (y_ref[rows, :] * _lanes(stats_ref[rows, :], width) * ng_ref[...]).astype(BF16)

    _row_blocks(n_rows, body)
    y = _dot(hy_ref[...].astype(BF16), wa_ref[...]) + _dot(b_ref[...], wb_ref[...])
    o_ref[...] = x_ref[...] + mod_ref[0, 5:6, :] * y


def _outproj_odd(hy, ys, p, z_block, norm_g, x, mods, mod_of_tile, w, tm):
    n_tok, d = x.shape
    half = hy.shape[1]
    return pl.pallas_call(
        _outproj_odd_kernel,
        grid=(n_tok // tm,),
        in_specs=[pl.BlockSpec((tm, half), lambda i: (i, 0)),
                  pl.BlockSpec((tm, half), lambda i: (i, 0)),
                  pl.BlockSpec((tm, half), lambda i: (i, z_block)),
                  pl.BlockSpec((1, half), lambda i: (0, 0)),
                  pl.BlockSpec((tm, d), lambda i: (i, 0)),
                  pl.BlockSpec((1, N_MOD, d), lambda i: (mod_of_tile(i), 0, 0)),
                  pl.BlockSpec((half, d), lambda i: (0, 0)),
                  pl.BlockSpec((half, d), lambda i: (1, 0))],
        out_specs=pl.BlockSpec((tm, d), lambda i: (i, 0)),
        out_shape=jax.ShapeDtypeStruct((n_tok, d), F32),
        scratch_shapes=[pltpu.VMEM((tm, half), BF16), pltpu.VMEM((tm, half), F32), pltpu.VMEM((tm, V7X_LANES), F32)],
        compiler_params=_cp(("parallel",)),
        name="outproj_odd",
    )(hy, ys, p, norm_g.reshape(1, half), x, mods, w, w)


def _s5_kernel(uc_ref, ul_ref, a_ref, bre_ref, bim_ref, cre_ref, cim_ref, yc_ref, yl_ref,
               s_ref, carry_ref, ub_ref, yb_ref, *, nc_c, tc, nb, kt_n, half):
    d = pl.program_id(0)
    j = pl.program_id(1)
    kin = bre_ref.shape[2]
    kst = bre_ref.shape[3]
    ln = ub_ref.shape[2]
    w = ub_ref.shape[0] * ln
    lpk = kin // ln

    @pl.when(j == 0)
    def _():
        carry_ref[...] = jnp.zeros_like(carry_ref)

    def run(u_ref, y_ref):
        for b in range(nb):
            for cb in range(w // ln):
                ub_ref[cb, pl.ds(b, tc, stride=nb), :] = u_ref[:, b * w + cb * ln:b * w + (cb + 1) * ln]
        for kt in range(kt_n):
            uk = jnp.concatenate([ub_ref[kt * lpk + i] for i in range(lpk)], axis=1).astype(BF16)
            s_ref[:, kt * kst:(kt + 1) * kst] = _dot(uk, bre_ref[0, kt])
            s_ref[:, half + kt * kst:half + (kt + 1) * kst] = _dot(uk, bim_ref[0, kt])
        for c0 in range(0, half, S5_W):
            a_re = jnp.broadcast_to(a_ref[0, 0:1, c0:c0 + S5_W], (nb, S5_W))
            a_im = jnp.broadcast_to(a_ref[0, 1:2, c0:c0 + S5_W], (nb, S5_W))

            def step(i, st):
                s_re, s_im = st
                t = jnp.where(d == 0, i, tc - 1 - i)
                row = pl.multiple_of(t * nb, nb)
                n_re = a_re * s_re - a_im * s_im + s_ref[pl.ds(row, nb), c0:c0 + S5_W]
                n_im = a_re * s_im + a_im * s_re + s_ref[pl.ds(row, nb), half + c0:half + c0 + S5_W]
                s_ref[pl.ds(row, nb), c0:c0 + S5_W] = n_re
                s_ref[pl.ds(row, nb), half + c0:half + c0 + S5_W] = n_im
                return n_re, n_im

            s_re, s_im = lax.fori_loop(0, tc, step, (carry_ref[:, c0:c0 + S5_W], carry_ref[:, half + c0:half + c0 + S5_W]),
                                       unroll=4)
            carry_ref[:, c0:c0 + S5_W] = s_re
            carry_ref[:, half + c0:half + c0 + S5_W] = s_im
        for kt in range(kt_n):
            s_re = s_ref[:, kt * kst:(kt + 1) * kst].astype(BF16)
            s_im = s_ref[:, half + kt * kst:half + (kt + 1) * kst].astype(BF16)
            yk = _dot(s_re, cre_ref[0, kt]) + _dot(s_im, cim_ref[0, kt])
            for i in range(lpk):
                yb_ref[kt * lpk + i] = yk[:, i * ln:(i + 1) * ln]
        for b in range(nb):
            for cb in range(w // ln):
                y_ref[0, :, b * w + cb * ln:b * w + (cb + 1) * ln] = yb_ref[cb, pl.ds(b, tc, stride=nb), :]

    @pl.when(j < nc_c)
    def _():
        run(uc_ref, yc_ref)

    @pl.when(j >= nc_c)
    def _():
        run(ul_ref, yl_ref)


def _s5_scan(uc, ul, a_bar, bre, bim, cre, cim, nb):
    lc, wide = uc.shape
    seq = ul.shape[0]
    w = wide // nb
    r = S5_TC * nb
    nc_c, nc_l = lc // S5_TC, seq // S5_TC
    kt_n = bre.shape[1]
    half = a_bar.shape[2]

    def c_idx(d, j):
        jc = jnp.minimum(j, nc_c - 1)
        return jnp.where(d == 0, jc, nc_c - 1 - jc)

    def l_idx(d, j):
        jl = jnp.maximum(j - nc_c, 0)
        return jnp.where(d == 0, jl, nc_l - 1 - jl)

    wspec = lambda shp: pl.BlockSpec((1,) + shp, lambda d, j: (d,) + (0,) * len(shp))
    return pl.pallas_call(
        functools.partial(_s5_kernel, nc_c=nc_c, tc=S5_TC, nb=nb, kt_n=kt_n, half=half),
        grid=(2, nc_c + nc_l),
        in_specs=[pl.BlockSpec((S5_TC, wide), lambda d, j: (c_idx(d, j), 0)),
                  pl.BlockSpec((S5_TC, wide), lambda d, j: (l_idx(d, j), 0)),
                  wspec(a_bar.shape[1:]), wspec(bre.shape[1:]), wspec(bim.shape[1:]),
                  wspec(cre.shape[1:]), wspec(cim.shape[1:])],
        out_specs=[pl.BlockSpec((1, S5_TC, wide), lambda d, j: (d, c_idx(d, j), 0)),
                   pl.BlockSpec((1, S5_TC, wide), lambda d, j: (d, l_idx(d, j), 0))],
        out_shape=[jax.ShapeDtypeStruct((2, lc, wide), F32), jax.ShapeDtypeStruct((2, seq, wide), F32)],
        scratch_shapes=[pltpu.VMEM((r, 2 * half), F32), pltpu.VMEM((nb, 2 * half), F32),
                        pltpu.VMEM((w // V7X_LANES, r, V7X_LANES), F32),
                        pltpu.VMEM((w // V7X_LANES, r, V7X_LANES), F32)],
        compiler_params=_cp(("arbitrary", "arbitrary")),
        name="s5_scan",
    )(uc, ul, a_bar, bre, bim, cre, cim)


def _s5_params(a_re, a_im, log_dt, b_re, b_im, c_re, c_im):
    l_re, l_im = a_re.astype(F32), a_im.astype(F32)
    dt = jnp.exp(log_dt.astype(F32))[..., None]
    mag = jnp.exp(l_re * dt)
    ab_re, ab_im = mag * jnp.cos(l_im * dt), mag * jnp.sin(l_im * dt)
    inv = 1.0 / (l_re * l_re + l_im * l_im)
    q_re = ((ab_re - 1.0) * l_re + ab_im * l_im) * inv
    q_im = (ab_im * l_re - (ab_re - 1.0) * l_im) * inv
    bb_re = q_re[..., None] * b_re.astype(F32) - q_im[..., None] * b_im.astype(F32)
    bb_im = q_re[..., None] * b_im.astype(F32) + q_im[..., None] * b_re.astype(F32)
    groups, p_dim, h_dim = bb_re.shape[1:]
    gl = V7X_LANES * 2 // h_dim
    kt = groups // gl
    eye = jnp.eye(gl, dtype=F32)
    a_pack = jnp.stack([ab_re.reshape(2, groups * p_dim), ab_im.reshape(2, groups * p_dim)], axis=1)

    def pack_b(x):
        x = x.reshape(2, kt, gl, p_dim, h_dim)
        return jnp.einsum('lm,dklph->dklhmp', eye, x).reshape(2, kt, gl * h_dim, gl * p_dim).astype(BF16)

    def pack_c(x):
        x = x.reshape(2, kt, gl, h_dim, p_dim)
        return jnp.einsum('lm,dklhp->dklpmh', eye, x).reshape(2, kt, gl * p_dim, gl * h_dim).astype(BF16)

    return (a_pack, pack_b(bb_re), pack_b(bb_im),
            pack_c(c_re.astype(F32)), pack_c(-c_im.astype(F32)))


def _s5_glu_kernel(u_ref, y_ref, d_ref, w_ref, b_ref, o_ref):
    y = d_ref[...] * u_ref[...] + y_ref[0] + y_ref[1]
    g = 0.5 * y * (1.0 + jnp.tanh(math.sqrt(2.0 / math.pi) * (y + 0.044715 * (y * y * y))))
    o_ref[...] = (g * _sigmoid(_dot(g.astype(BF16), w_ref[...]) + b_ref[...])).astype(BF16)


def _s5_glu(u_tm, y_tm, d_skip, glu_w, glu_b, seq, nb, tm):
    w = glu_w.shape[0]
    tps = seq // tm
    return pl.pallas_call(
        _s5_glu_kernel,
        grid=(nb, tps),
        in_specs=[pl.BlockSpec((tm, w), lambda b, i: (i, b)),
                  pl.BlockSpec((2, tm, w), lambda b, i: (0, i, b)),
                  pl.BlockSpec((1, w), lambda b, i: (0, 0)),
                  pl.BlockSpec((w, w), lambda b, i: (0, 0)),
                  pl.BlockSpec((1, w), lambda b, i: (0, 0))],
        out_specs=pl.BlockSpec((tm, w), lambda b, i: (b * tps + i, 0)),
        out_shape=jax.ShapeDtypeStruct((nb * seq, w), BF16),
        compiler_params=_cp(("parallel", "arbitrary")),
        name="s5_glu",
    )(u_tm, y_tm, d_skip.reshape(1, w), glu_w, glu_b.reshape(1, w))


def _na_groups(rows):
    wr = NA_GQ + NA_KH - 1
    plan, sigs = [], {}
    for g in range(rows // NA_GQ):
        r0 = g * NA_GQ
        ws = min(max(r0 - NA_KH // 2, 0), rows - wr)
        sig = tuple(min(max(r - NA_KH // 2, 0), rows - NA_KH) - ws for r in range(r0, r0 + NA_GQ)) + (r0 - ws,)
        tid = sigs.setdefault(sig, len(sigs))
        plan.append((r0, ws, tid))
    reps = [next(p for p in plan if p[2] == t) for t in range(len(sigs))]
    return plan, reps, wr


def _na_bias(rpb, rows):
    _, reps, wr = _na_groups(rows)
    col = np.arange(GRID_W)
    cs = np.clip(col - NA_KW // 2, 0, GRID_W - NA_KW)
    col_ok = (col[None, :] >= cs[:, None]) & (col[None, :] < cs[:, None] + NA_KW)
    dc = np.clip(col[None, :] - col[:, None] + NA_KW - 1, 0, 2 * NA_KW - 2)
    pick = (dc[None] == np.arange(2 * NA_KW - 1)[:, None, None]).astype(np.float32)
    tcol = jnp.einsum('hdc,cqk->hdqk', rpb.astype(F32), pick, precision=lax.Precision.HIGHEST)
    tcol = jnp.where(col_ok[None, None], tcol, NEG)
    dead = jnp.full(tcol[:, 0].shape, NEG, F32)
    tabs = []
    for r0, ws, _ in reps:
        row_blocks = []
        for r in range(r0, r0 + NA_GQ):
            rs = min(max(r - NA_KH // 2, 0), rows - NA_KH)
            blocks = [tcol[:, kr - r + NA_KH - 1] if rs <= kr < rs + NA_KH else dead for kr in range(ws, ws + wr)]
            row_blocks.append(jnp.concatenate(blocks, axis=-1))
        tabs.append(jnp.concatenate(row_blocks, axis=-2))
    return jnp.stack(tabs)


def _na_kernel(q_ref, k_ref, v_ref, qc_ref, kc_ref, vc_ref, bias_ref, o_ref, oc_ref, *, plan, wr, scale):
    kc = kc_ref[0]
    vc = vc_ref[0]
    nq = NA_GQ * GRID_W
    nk = wr * GRID_W
    for r0, ws, tid in plan:
        q = q_ref[0, r0 * GRID_W:r0 * GRID_W + nq, :]
        kw = k_ref[0, ws * GRID_W:ws * GRID_W + nk, :]
        vw = v_ref[0, ws * GRID_W:ws * GRID_W + nk, :]
        s1 = _dot_nt(q, kw) * scale + bias_ref[tid, 0]
        s2 = _dot_nt(q, kc) * scale
        m = jnp.maximum(jnp.max(s1, axis=-1, keepdims=True), jnp.max(s2, axis=-1, keepdims=True))
        p1 = jnp.exp(s1 - m)
        p2 = jnp.exp(s2 - m)
        den = jnp.sum(p1, axis=-1, keepdims=True) + jnp.sum(p2, axis=-1, keepdims=True)
        o = (_dot(p1.astype(BF16), vw) + _dot(p2.astype(BF16), vc)) / den
        o_ref[0, r0 * GRID_W:r0 * GRID_W + nq, :] = o.astype(BF16)
    s = _dot_nt(qc_ref[0], kc) * scale
    p = jnp.exp(s - jnp.max(s, axis=-1, keepdims=True))
    oc = _dot(p.astype(BF16), vc) / jnp.sum(p, axis=-1, keepdims=True)
    oc_ref[0] = oc.astype(BF16)


def _na_attention(qkv_l, qkv_c, bias, heads):
    bsz, seq, _ = qkv_l.shape
    lc = qkv_c.shape[1]
    hd = NA_HEAD_DIM
    plan, _, wr = _na_groups(seq // GRID_W)
    n_types = bias.shape[0]
    spec = lambda t, off: pl.BlockSpec((1, t, hd), lambda b, h: (b, 0, off + h))
    return pl.pallas_call(
        functools.partial(_na_kernel, plan=plan, wr=wr, scale=hd ** -0.5),
        grid=(bsz, heads),
        in_specs=[spec(seq, 0), spec(seq, heads), spec(seq, 2 * heads),
                  spec(lc, 0), spec(lc, heads), spec(lc, 2 * heads),
                  pl.BlockSpec((n_types, 1) + bias.shape[2:], lambda b, h: (0, h, 0, 0))],
        out_specs=[pl.BlockSpec((1, seq, hd), lambda b, h: (b, 0, h)),
                   pl.BlockSpec((1, lc, hd), lambda b, h: (b, 0, h))],
        out_shape=[jax.ShapeDtypeStruct((bsz, seq, heads * hd), BF16),
                   jax.ShapeDtypeStruct((bsz, lc, heads * hd), BF16)],
        compiler_params=_cp(("parallel", "arbitrary")),
        name="nbr_attention",
    )(qkv_l, qkv_l, qkv_l, qkv_c, qkv_c, qkv_c, bias)


def _dwconv3(load, t_len, r0, rows, w, b):
    main = load(r0, rows)
    idx = lax.broadcasted_iota(jnp.int32, main.shape, 0)
    prev = load(r0 - 1, 1) if r0 > 0 else jnp.zeros_like(main[0:1])
    nxt = load(r0 + rows, 1) if r0 + rows < t_len else jnp.zeros_like(main[0:1])
    up = jnp.where(idx == 0, prev, pltpu.roll(main, 1, 0))
    down = jnp.where(idx == rows - 1, nxt, pltpu.roll(main, rows - 1, 0))
    return w[0:1] * up + w[1:2] * main + w[2:3] * down + b


CONV_ROWS = 256


def _ssd_kernel(xc_ref, bc_ref, cc_ref, dtc_ref, xl_ref, bl_ref, cl_ref, dtl_ref,
                wx_ref, wb_ref, wc_ref, bx_ref, bb_ref, bcb_ref, dtb_ref, a_ref, dsk_ref,
                y_ref, xs_ref, bs_ref, cs_ref, dts_ref, cum_ref, st_ref, dec_ref, *, lc, seq, e_n):
    q = SSD_CHUNK
    p = SSD_HEAD_DIM
    ncc, ncl = lc // q, seq // q
    nch = ncc + ncl

    for src, dst, w_ref, b_ref in ((0, xs_ref, wx_ref, bx_ref), (1, bs_ref, wb_ref, bb_ref), (2, cs_ref, wc_ref, bcb_ref)):
        w = w_ref[...]
        b = b_ref[...]
        for base, t_len, ref in ((0, lc, (xc_ref, bc_ref, cc_ref)[src]), (lc, seq, (xl_ref, bl_ref, cl_ref)[src])):
            cr = min(CONV_ROWS, t_len)
            for r0 in range(0, t_len, cr):
                dst[base + r0:base + r0 + cr, :] = _silu(
                    _dwconv3(lambda a, n, ref=ref: ref[0, a:a + n, :], t_len, r0, cr, w, b))
    for base, t_len, ref in ((0, lc, dtc_ref), (lc, seq, dtl_ref)):
        v = ref[0] + dtb_ref[...]
        dts_ref[base:base + t_len, :] = jnp.maximum(v, 0.0) + jnp.log(1.0 + jnp.exp(-jnp.abs(v)))

    ii = lax.broadcasted_iota(jnp.int32, (q, q), 0)
    jj = lax.broadcasted_iota(jnp.int32, (q, q), 1)
    keep = (jj <= ii, jj >= ii)
    tri = tuple(jnp.where(m, 1.0, 0.0).astype(BF16) for m in keep)
    a_row = a_ref[...]
    dsk = dsk_ref[...]
    xw = e_n * p
    fwd_lane = lax.broadcasted_iota(jnp.int32, (q, V7X_LANES), 1) < e_n
    spread = jnp.where(lax.broadcasted_iota(jnp.int32, (V7X_LANES, 2 * xw), 0)
                       == lax.broadcasted_iota(jnp.int32, (V7X_LANES, 2 * xw), 1) // p, 1.0, 0.0).astype(BF16)

    def expand(v):
        hi, lo = _split2(v)
        return _dot(hi, spread) + _dot(lo, spread)

    def local(c, carry):
        r0 = pl.multiple_of(c * q, q)
        x = xs_ref[pl.ds(r0, q), :]
        dt = dts_ref[pl.ds(r0, q), :]
        dta = dt * a_row
        d_hi = dta.astype(BF16)
        r1 = dta - d_hi.astype(F32)
        d_mid = r1.astype(BF16)
        d_lo = (r1 - d_mid.astype(F32)).astype(BF16)
        bm_t = bs_ref[pl.ds(r0, q), :].T.astype(BF16)
        cum = [_dot(tri[k], d_hi) + (_dot(tri[k], d_mid) + _dot(tri[k], d_lo)) for k in range(2)]
        cum = jnp.where(fwd_lane, cum[0], cum[1])
        cum_ref[pl.ds(r0, q), :] = cum
        tot = jnp.where(fwd_lane[0:1], cum[q - 1:q, :], cum[0:1, :])
        wgt = expand(dt * jnp.exp(tot - cum))
        st = _dot(bm_t, (jnp.concatenate([x, x], axis=1) * wgt).astype(BF16))
        dec = expand(jnp.broadcast_to(jnp.exp(tot), (V7X_SUBLANES, V7X_LANES)))[0:1]
        for k in range(2):
            st_ref[k, c] = st[:, k * xw:(k + 1) * xw]
            dec_ref[k, c] = dec[:, k * xw:(k + 1) * xw]
        return carry

    lax.fori_loop(0, nch, local, 0, unroll=2)

    for k in range(2):
        def prop(i, s, k=k):
            c = i if k == 0 else jnp.where(i < ncc, ncc - 1 - i, nch - 1 - (i - ncc))
            loc = st_ref[k, c]
            st_ref[k, c] = s
            return s * dec_ref[k, c] + loc

        lax.fori_loop(0, nch, prop, jnp.zeros(st_ref.shape[2:], F32))

    def emit(i, carry):
        c = ncc + i
        r0 = pl.multiple_of(c * q, q)
        x = xs_ref[pl.ds(r0, q), :]
        dt = dts_ref[pl.ds(r0, q), :]
        cm = cs_ref[pl.ds(r0, q), :].astype(BF16)
        gmat = _dot_nt(cm, bs_ref[pl.ds(r0, q), :].astype(BF16))
        cum = cum_ref[pl.ds(r0, q), :]
        cum_t = cum.T
        xdt = (jnp.concatenate([x, x], axis=1) * expand(dt)).astype(BF16)
        ydiag = []
        for e in range(e_n):
            lhs, rhs = [], []
            for k in range(2):
                col = k * e_n + e
                seg = jnp.where(keep[k], cum[:, col:col + 1] - cum_t[col:col + 1, :], NEG)
                lhs.append((gmat * jnp.exp(seg)).astype(BF16))
                rhs.append(xdt[:, k * xw + e * p:k * xw + (e + 1) * p])
            ydiag.append(_dot(jnp.concatenate(lhs, axis=1), jnp.concatenate(rhs, axis=0)))
        y = jnp.concatenate(ydiag, axis=1) + dsk * x
        eoff = expand(jnp.exp(cum))
        for k in range(2):
            y = y + _dot(cm, st_ref[k, c].astype(BF16)) * eoff[:, k * xw:(k + 1) * xw]
        y_ref[0, pl.ds(pl.multiple_of(i * q, q), q), :] = y
        return carry

    lax.fori_loop(0, ncl, emit, 0, unroll=2)


def _ssd(p_l, dt_l, p_c, dt_c, x_off_l, x_off_c, conv_w, conv_b, dtb, a_neg, d_exp, inner):
    bsz, seq, _ = p_l.shape
    lc = p_c.shape[1]
    e_n = inner // SSD_HEAD_DIM // SSD_GROUPS
    xw = e_n * SSD_HEAD_DIM
    n = SSD_STATE
    t_tot = lc + seq

    def specs(t, off):
        return [pl.BlockSpec((1, t, xw), lambda b, g: (b, 0, off // xw + g)),
                pl.BlockSpec((1, t, n), lambda b, g: (b, 0, (off + inner) // n + g)),
                pl.BlockSpec((1, t, n), lambda b, g: (b, 0, (off + inner) // n + SSD_GROUPS + g)),
                pl.BlockSpec((1, t, V7X_LANES), lambda b, g: (b, 0, g))]

    def pspecs(rows):
        return [pl.BlockSpec((rows, xw), lambda b, g: (0, g)),
                pl.BlockSpec((rows, n), lambda b, g: (0, inner // n + g)),
                pl.BlockSpec((rows, n), lambda b, g: (0, inner // n + SSD_GROUPS + g))]

    lane = lambda: pl.BlockSpec((1, V7X_LANES), lambda b, g: (0, g))
    return pl.pallas_call(
        functools.partial(_ssd_kernel, lc=lc, seq=seq, e_n=e_n),
        grid=(bsz, SSD_GROUPS),
        in_specs=specs(lc, x_off_c) + specs(seq, x_off_l) + pspecs(3) + pspecs(1)
        + [lane(), lane(), pl.BlockSpec((1, xw), lambda b, g: (0, g))],
        out_specs=pl.BlockSpec((1, seq, xw), lambda b, g: (b, 0, g)),
        out_shape=jax.ShapeDtypeStruct((bsz, seq, inner), F32),
        scratch_shapes=[pltpu.VMEM((t_tot, xw), F32), pltpu.VMEM((t_tot, n), F32), pltpu.VMEM((t_tot, n), F32),
                        pltpu.VMEM((t_tot, V7X_LANES), F32), pltpu.VMEM((t_tot, V7X_LANES), F32),
                        pltpu.VMEM((2, t_tot // SSD_CHUNK, n, xw), F32), pltpu.VMEM((2, t_tot // SSD_CHUNK, 1, xw), F32)],
        compiler_params=_cp(("parallel", "arbitrary")),
        name="ssd",
    )(p_c, p_c, p_c, dt_c, p_l, p_l, p_l, dt_l, conv_w, conv_w, conv_w,
      conv_b, conv_b, conv_b, dtb, a_neg, d_exp)


def _hy_filter_kernel(z_ref, w_in_ref, b_in_ref, w_mid_ref, b_mid_ref, w_out_ref, freq_ref, delta_ref, o_ref, h_ref):
    j = pl.program_id(1)
    hp = lax.Precision.HIGHEST

    @pl.when(j == 0)
    def _():
        freq = freq_ref[...]
        h = jnp.sin(freq * (jnp.dot(z_ref[...], w_in_ref[...], precision=hp, preferred_element_type=F32) + b_in_ref[...]))
        for i in range(w_mid_ref.shape[0]):
            h = jnp.sin(freq * (jnp.dot(h, w_mid_ref[i], precision=hp, preferred_element_type=F32) + b_mid_ref[i]))
        h_ref[...] = h

    decay = jnp.exp(-z_ref[:, 0:1] * delta_ref[...])
    o_ref[...] = jnp.dot(h_ref[...], w_out_ref[...], precision=hp, preferred_element_type=F32) * decay


def _hy_filters(seq, w_in, b_in, w_mid, b_mid, w_out, freq, width):
    t = jnp.linspace(0.0, 1.0, seq, dtype=F32)[:, None]
    w = 2.0 * math.pi * jnp.arange(seq, dtype=F32)[:, None] / seq
    f = jnp.linspace(1e-4, HY_BANDS - 1, HY_BANDS, dtype=F32)[None, :]
    z = jnp.concatenate([t, jnp.cos(f * w), -jnp.sin(f * w)], axis=-1)
    emb, hid = w_in.shape
    pad = V7X_LANES
    z = jnp.pad(z, ((0, 0), (0, pad - emb)))
    w_in_p = jnp.pad(w_in.astype(F32), ((0, pad - emb), (0, pad - hid)))
    padv = lambda v: jnp.pad(v.astype(F32), ((0, 0), (0, pad - hid)))
    w_mid_p = jnp.pad(w_mid.astype(F32), ((0, 0), (0, pad - hid), (0, pad - hid)))
    w_out_p = jnp.pad(w_out.astype(F32), ((0, pad - hid), (0, 0)))
    n_out = w_out.shape[1]
    max_decay = math.log(HY_DECAY_TARGET) / HY_FAST_PCT
    min_decay = math.log(HY_DECAY_TARGET) / HY_SLOW_PCT
    deltas = jnp.abs(jnp.linspace(min_decay, max_decay, width, dtype=F32))[None, :]
    tr = 256
    n_mid = w_mid.shape[0]
    return pl.pallas_call(
        _hy_filter_kernel,
        grid=(seq // tr, n_out // width),
        in_specs=[pl.BlockSpec((tr, pad), lambda i, j: (i, 0)),
                  pl.BlockSpec((pad, pad), lambda i, j: (0, 0)),
                  pl.BlockSpec((1, pad), lambda i, j: (0, 0)),
                  pl.BlockSpec((n_mid, pad, pad), lambda i, j: (0, 0, 0)),
                  pl.BlockSpec((n_mid, 1, pad), lambda i, j: (0, 0, 0)),
                  pl.BlockSpec((pad, width), lambda i, j: (0, j)),
                  pl.BlockSpec((1, pad), lambda i, j: (0, 0)),
                  pl.BlockSpec((1, width), lambda i, j: (0, 0))],
        out_specs=pl.BlockSpec((tr, width), lambda i, j: (i, j)),
        out_shape=jax.ShapeDtypeStruct((seq, n_out), F32),
        scratch_shapes=[pltpu.VMEM((tr, pad), F32)],
        compiler_params=_cp(("parallel", "arbitrary")),
        name="hyena_filter",
    )(z, w_in_p, padv(b_in[None]), w_mid_p, padv(b_mid)[:, None, :], w_out_p, padv(freq[None]), deltas)


def _dft_tables(seq):
    n = 2 * seq
    k = jnp.arange(seq, dtype=jnp.int32)
    blk = DFT_BLOCK
    th = 2.0 * math.pi / n
    ang_a = th * ((blk * k[:seq // blk, None] * k[None, :]) % n).astype(F32)
    ang_b = th * ((k[:blk, None] * k[None, :]) % n).astype(F32)
    ca, sa = jnp.cos(ang_a)[:, None, :], jnp.sin(ang_a)[:, None, :]
    cb, sb = jnp.cos(ang_b)[None], jnp.sin(ang_b)[None]
    cos = (ca * cb - sa * sb).reshape(seq, seq)
    sin = (sa * cb + ca * sb).reshape(seq, seq)
    alt = jnp.where(k % 2 == 0, 1.0, -1.0).astype(F32)
    sf = jnp.where(k[:, None] == 0, alt[None, :], sin)
    wk = jnp.where(k == 0, 1.0, 2.0).astype(F32) / n
    ci = cos * wk[None, :]
    si = jnp.where(k[None, :] == 0, alt[:, None] / n, sin * (2.0 / n))
    return _split2(cos) + _split2(sf) + (ci.astype(BF16), si.astype(BF16))


def _hy_spec_kernel(hf_ref, hb_ref, cfh_ref, cfl_ref, sfh_ref, sfl_ref, kr_ref, ki_ref,
                    smh_ref, sml_ref, dfh_ref, dfl_ref, nyq_ref):
    k = pl.program_id(1)

    @pl.when(k == 0)
    def _():
        hf = hf_ref[...]
        t = lax.broadcasted_iota(jnp.int32, hf.shape, 0)
        hb = jnp.where(t == 0, 0.0, hb_ref[...])
        sm = hf + hb
        df = hf - hb
        smh_ref[...], sml_ref[...] = _split2(sm)
        dfh_ref[...], dfl_ref[...] = _split2(df)
        nyq_ref[...] = jnp.sum(jnp.where(t % 2 == 0, sm, -sm), axis=0, keepdims=True)

    kr_ref[...] = _dot3(cfh_ref[...], cfl_ref[...], smh_ref[...], sml_ref[...])
    ki = _dot3(sfh_ref[...], sfl_ref[...], dfh_ref[...], dfl_ref[...])
    row = lax.broadcasted_iota(jnp.int32, ki.shape, 0)
    ki_ref[...] = jnp.where((row == 0) & (k == 0), nyq_ref[...], ki)


def _hy_spectra(h, tabs, width):
    seq, n_cols = h.shape
    order = n_cols // (2 * width)
    cpb = width // HY_TC
    cfh, cfl, sfh, sfl = tabs[:4]
    fspec = lambda: pl.BlockSpec((HY_TK, seq), lambda c, k: (k, 0))
    return pl.pallas_call(
        _hy_spec_kernel,
        grid=(order * cpb, seq // HY_TK),
        in_specs=[pl.BlockSpec((seq, HY_TC), lambda c, k: (0, (c // cpb) * 2 * cpb + c % cpb)),
                  pl.BlockSpec((seq, HY_TC), lambda c, k: (0, (c // cpb) * 2 * cpb + cpb + c % cpb)),
                  fspec(), fspec(), fspec(), fspec()],
        out_specs=[pl.BlockSpec((HY_TK, HY_TC), lambda c, k: (k, c)),
                   pl.BlockSpec((HY_TK, HY_TC), lambda c, k: (k, c))],
        out_shape=[jax.ShapeDtypeStruct((seq, order * width), F32)] * 2,
        scratch_shapes=[pltpu.VMEM((seq, HY_TC), BF16)] * 4 + [pltpu.VMEM((1, HY_TC), F32)],
        compiler_params=_cp(("parallel", "arbitrary")),
        name="hyena_spectra",
    )(h, h, cfh, cfl, sfh, sfl)


def _hy_conv_kernel(u_ref, g_ref, wu_ref, bu_ref, wg_ref, bg_ref, kr_ref, ki_ref, fb_ref,
                    cf_ref, sf_ref, ci_ref, si_ref,
                    o_ref, ub_ref, acc_ref, *, conv_u, seq):
    k = pl.program_id(2)
    cr = CONV_ROWS

    def u_rows(r0):
        if conv_u:
            return _dwconv3(lambda a, n: u_ref[0, a:a + n, :], seq, r0, cr, wu_ref[...], bu_ref[...])
        return u_ref[0, r0:r0 + cr, :]

    @pl.when(k == 0)
    def _():
        for r0 in range(0, seq, cr):
            ub_ref[r0:r0 + cr, :] = u_rows(r0).astype(BF16)
        acc_ref[...] = jnp.zeros_like(acc_ref)

    ub = ub_ref[...]
    a = _dot(cf_ref[...], ub)
    b = _dot(sf_ref[...], ub)
    kr = kr_ref[...]
    ki = ki_ref[...]
    first = (lax.broadcasted_iota(jnp.int32, a.shape, 0) == 0) & (k == 0)
    bki = b * ki
    yc = a * kr - jnp.where(first, 0.0, bki)
    ys = jnp.where(first, bki, a * ki + b * kr)
    acc_ref[...] += _dot(ci_ref[...], yc.astype(BF16)) + _dot(si_ref[...], ys.astype(BF16))

    @pl.when(k == pl.num_programs(2) - 1)
    def _():
        for r0 in range(0, seq, cr):
            gate = _dwconv3(lambda a_, n: g_ref[0, a_:a_ + n, :], seq, r0, cr, wg_ref[...], bg_ref[...])
            o_ref[0, r0:r0 + cr, :] = gate * (acc_ref[r0:r0 + cr, :] + u_rows(r0) * fb_ref[0])


def _hy_conv(u, u_blk, g, g_blk, short_w, short_b, wu_blk, wg_blk, kr, ki, k_blk, fbias, order, tabs, width, conv_u):
    bsz, seq, _ = g.shape
    cpb = width // HY_TC
    fspec = lambda: pl.BlockSpec((HY_TK, seq), lambda b, c, k: (k, 0))
    ispec = lambda: pl.BlockSpec((seq, HY_TK), lambda b, c, k: (0, k))
    once = dict(pipeline_mode=pl.Buffered(1))
    return pl.pallas_call(
        functools.partial(_hy_conv_kernel, conv_u=conv_u, seq=seq),
        grid=(bsz, cpb, seq // HY_TK),
        in_specs=[pl.BlockSpec((1, seq, HY_TC), lambda b, c, k: (b, 0, u_blk + c), **once),
                  pl.BlockSpec((1, seq, HY_TC), lambda b, c, k: (b, 0, g_blk + c), **once),
                  pl.BlockSpec((3, HY_TC), lambda b, c, k: (0, wu_blk + c)),
                  pl.BlockSpec((1, HY_TC), lambda b, c, k: (0, wu_blk + c)),
                  pl.BlockSpec((3, HY_TC), lambda b, c, k: (0, wg_blk + c)),
                  pl.BlockSpec((1, HY_TC), lambda b, c, k: (0, wg_blk + c)),
                  pl.BlockSpec((HY_TK, HY_TC), lambda b, c, k: (k, k_blk + c)),
                  pl.BlockSpec((HY_TK, HY_TC), lambda b, c, k: (k, k_blk + c)),
                  pl.BlockSpec((1, 1, HY_TC), lambda b, c, k: (order, 0, c)),
                  fspec(), fspec(), ispec(), ispec()],
        out_specs=pl.BlockSpec((1, seq, HY_TC), lambda b, c, k: (b, 0, c)),
        out_shape=jax.ShapeDtypeStruct((bsz, seq, width), F32),
        scratch_shapes=[pltpu.VMEM((seq, HY_TC), BF16), pltpu.VMEM((seq, HY_TC), F32)],
        compiler_params=_cp(("parallel", "parallel", "arbitrary")),
        name="hyena_conv",
    )(u, g, short_w, short_b, short_w, short_b, kr, ki, fbias, tabs[0], tabs[2], tabs[4], tabs[5])


def kernel(x, c, ctx, c_ctx, mod_w, mod_b, norm_g, ffn_wg, ffn_wu, ffn_wd, final_g,
           ev_w_in, ev_w_out, s5_a_re, s5_a_im, s5_log_dt, s5_b_re, s5_b_im, s5_c_re, s5_c_im,
           s5_d, s5_glu_w, s5_glu_b, na_rpb,
           od_w_in, od_w_out, hy_short_w, hy_short_b, hy_w_in, hy_b_in, hy_w_mid, hy_b_mid,
           hy_w_out, hy_freq, hy_fbias,
           ssd_conv_w, ssd_conv_b, ssd_dt_bias, ssd_a_log, ssd_d, ssd_norm_g):
    bsz, seq, d = x.shape
    lc = ctx.shape[1]
    depth = mod_w.shape[0]
    assert depth == 2 and bsz == V7X_SUBLANES, "layer 0 = S5 || attention with context output, layer 1 = Hyena || SSD"
    assert seq % TM_WIDE == 0 and (bsz * lc) % TM == 0 and lc % CONV_ROWS == 0

    xl = x.reshape(bsz * seq, d)
    xc = ctx.reshape(bsz * lc, d)
    tps = seq // TM
    mod_l = lambda i: i // tps
    mod_c = lambda i: bsz

    cond = jnp.concatenate([c, c_ctx[None], jnp.zeros((V7X_SUBLANES - 1, d), F32)], axis=0)
    mods = _modulation(cond, mod_w, mod_b).reshape(depth, cond.shape[0], N_MOD, d)

    wg, wu, wd = ffn_wg.astype(BF16), ffn_wu.astype(BF16), ffn_wd.astype(BF16)

    def ffn(xx, layer, half, mod_of, final=False):
        return _ffn(xx, mods[layer], mod_of, norm_g[layer, 2 * half], wg, wu, wd, layer, half, final_g, final, TM)

    xl = ffn(xl, 0, 0, mod_l)
    xc = ffn(xc, 0, 0, mod_c)
    w_in = ev_w_in[0].astype(BF16)
    s5w = s5_d.shape[1] * s5_d.shape[2]
    u_l, qkv_l = _inproj_even(xl, mods[0], mod_l, norm_g[0, 1], w_in, s5w, seq, bsz, TM)
    u_c, qkv_c = _inproj_even(xc, mods[0], mod_c, norm_g[0, 1], w_in, s5w, lc, bsz, lc)
    s5p = _s5_params(s5_a_re[0], s5_a_im[0], s5_log_dt[0], s5_b_re[0], s5_b_im[0], s5_c_re[0], s5_c_im[0])
    y_c, y_l = _s5_scan(u_c, u_l, *s5p, bsz)
    glu_w = s5_glu_w[0].astype(BF16)
    a_l = _s5_glu(u_l, y_l, s5_d[0], glu_w, s5_glu_b[0], seq, bsz, TM)
    a_c = _s5_glu(u_c, y_c, s5_d[0], glu_w, s5_glu_b[0], lc, bsz, lc)
    heads = na_rpb.shape[1]
    bias = _na_bias(na_rpb[0], seq // GRID_W)
    o_l, o_c = _na_attention(qkv_l.reshape(bsz, seq, -1), qkv_c.reshape(bsz, lc, -1), bias, heads)
    w_out = ev_w_out[0].astype(BF16)
    xl = _outproj_even(a_l, o_l.reshape(bsz * seq, -1), xl, mods[0], mod_l, w_out, TM)
    xc = _outproj_even(a_c, o_c.reshape(bsz * lc, -1), xc, mods[0], mod_c, w_out, TM)
    xl = ffn(xl, 0, 1, mod_l)
    xc = ffn(xc, 0, 1, mod_c)

    xl = ffn(xl, 1, 0, mod_l)
    xc = ffn(xc, 1, 0, mod_c)
    hyw = hy_fbias.shape[2]
    inner = ssd_norm_g.shape[1]
    n_heads = ssd_d.shape[1]
    e_n = n_heads // SSD_GROUPS
    o_z = 3 * hyw
    o_xbc = o_z + inner
    o_dt = o_xbc + inner + 2 * SSD_GROUPS * SSD_STATE
    w_od = od_w_in[0]
    w_dt = w_od[:, o_dt:].reshape(d, 2, SSD_GROUPS, e_n)
    w_dt = jnp.transpose(w_dt, (0, 2, 1, 3)).reshape(d, SSD_GROUPS, 2 * e_n)
    w_dt = jnp.pad(w_dt, ((0, 0), (0, 0), (0, V7X_LANES - 2 * e_n))).reshape(d, SSD_GROUPS * V7X_LANES).astype(BF16)

    def regroup(v):
        v = jnp.transpose(v.astype(F32).reshape(2, SSD_GROUPS, e_n), (1, 0, 2)).reshape(SSD_GROUPS, 2 * e_n)
        return jnp.pad(v, ((0, 0), (0, V7X_LANES - 2 * e_n))).reshape(1, SSD_GROUPS * V7X_LANES)

    w_main = w_od[:, :o_dt].astype(BF16)
    p_l, dt_l = _inproj_odd(xl, mods[1], lambda i: i // (seq // TM_WIDE), norm_g[1, 1], w_main, w_dt, TM_WIDE)
    p_c, dt_c = _inproj_odd(xc, mods[1], mod_c, norm_g[1, 1], w_main[:, o_xbc:], w_dt, TM)
    p_l3 = p_l.reshape(bsz, seq, -1)
    y_ssd = _ssd(p_l3, dt_l.reshape(bsz, seq, -1), p_c.reshape(bsz, lc, -1), dt_c.reshape(bsz, lc, -1),
                 o_xbc, 0, ssd_conv_w[0], ssd_conv_b[0][None], regroup(ssd_dt_bias[0]),
                 regroup(-jnp.exp(ssd_a_log[0].astype(F32))),
                 jnp.repeat(ssd_d[0].astype(F32), SSD_HEAD_DIM)[None], inner)

    filt = _hy_filters(seq, hy_w_in[0], hy_b_in[0], hy_w_mid[0], hy_b_mid[0], hy_w_out[0], hy_freq[0], hyw)
    tabs = _dft_tables(seq)
    kr, ki = _hy_spectra(filt, tabs, hyw)
    cpb = hyw // HY_TC
    sw, sb = hy_short_w[0], hy_short_b[0][None]
    fb = hy_fbias[0][:, None, :]
    z1 = _hy_conv(p_l3, 2 * cpb, p_l3, 0, sw, sb, 2 * cpb, 0, kr, ki, 0, fb, 0, tabs, hyw, True)
    y_hy = _hy_conv(z1, 0, p_l3, cpb, sw, sb, 0, cpb, kr, ki, cpb, fb, 1, tabs, hyw, False)

    xl = _outproj_odd(y_hy.reshape(bsz * seq, hyw), y_ssd.reshape(bsz * seq, inner), p_l, o_z // inner,
                      ssd_norm_g[0], xl, mods[1], mod_l, od_w_out[0].astype(BF16), TM)
    xl = ffn(xl, 1, 1, mod_l, final=True)
    return xl.reshape(bsz, seq, d)
```

```python
import functools
import math

import numpy as np
import jax
import jax.numpy as jnp
from jax import lax
from jax.experimental import pallas as pl
from jax.experimental.pallas import tpu as pltpu

F32 = jnp.float32
BF16 = jnp.bfloat16

EPS = 1e-6
GRID_W = 64
N_MOD = 9
S5_GROUP = 16
S5_STATE = 64
NA_HEAD_DIM = 128
NA_KH = 8
NA_KW = 16
HY_BANDS = 16
HY_DECAY_TARGET = 1e-2
HY_FAST_PCT = 0.3
HY_SLOW_PCT = 1.5
SSD_HEAD_DIM = 64
SSD_GROUPS = 4
SSD_STATE = 128
SSD_CHUNK = 128

V7X_LANES = 128
V7X_SUBLANES = 8
V7X_VMEM_BYTES = 64 * 1024 * 1024
VMEM_LIMIT = V7X_VMEM_BYTES - 12 * 1024 * 1024

TM = 512
TM_WIDE = 1024
ROW_BLOCK = 16
TF = 512
TN = 1024
S5_TC = 32
S5_W = 512
NA_GQ = 4
HY_TC = 512
HY_TK = 512
DFT_BLOCK = 64
NEG = -1e30


def _cp(sem, vmem=VMEM_LIMIT):
    return pltpu.CompilerParams(dimension_semantics=sem, vmem_limit_bytes=vmem)


def _sigmoid(x):
    return 1.0 / (1.0 + jnp.exp(-x))


def _silu(x):
    return x * _sigmoid(x)


def _dot(a, b):
    return jnp.dot(a, b, preferred_element_type=F32)


def _dot_nt(a, b):
    return lax.dot_general(a, b, (((1,), (1,)), ((), ())), preferred_element_type=F32)


def _split2(x):
    hi = x.astype(BF16)
    lo = (x - hi.astype(F32)).astype(BF16)
    return hi, lo


def _dot3(a_hi, a_lo, b_hi, b_lo):
    return _dot(a_hi, b_hi) + (_dot(a_hi, b_lo) + _dot(a_lo, b_hi))


def _rms(x):
    return x * lax.rsqrt(jnp.mean(x * x, axis=-1, keepdims=True) + EPS)


def _row_blocks(n_rows, body):
    def step(r, c):
        body(pl.ds(pl.multiple_of(r * ROW_BLOCK, ROW_BLOCK), ROW_BLOCK))
        return c

    lax.fori_loop(0, n_rows // ROW_BLOCK, step, 0, unroll=2)


def _rms_scale_into(row_fn, stats_ref, n_rows, width):
    def part(rows):
        sq = row_fn(rows)
        sq = sq * sq
        acc = sq[:, 0:V7X_LANES]
        for j in range(1, width // V7X_LANES):
            acc = acc + sq[:, j * V7X_LANES:(j + 1) * V7X_LANES]
        stats_ref[rows, :] = acc

    _row_blocks(n_rows, part)
    tot = jnp.sum(stats_ref[...], axis=-1, keepdims=True)
    stats_ref[...] = jnp.broadcast_to(lax.rsqrt(tot * (1.0 / width) + EPS), stats_ref.shape)


def _lanes(stat, width):
    return jnp.concatenate([stat] * (width // V7X_LANES), axis=1)


def _prenorm_into(x_ref, xn_ref, g_ref, mod_ref, sub, stats_ref, aff_ref):
    n_rows, width = x_ref.shape
    _rms_scale_into(lambda rows: x_ref[rows, :], stats_ref, n_rows, width)
    gain = g_ref[...] * (1.0 + mod_ref[0, 3 * sub + 1:3 * sub + 2, :])
    aff_ref[0] = jnp.broadcast_to(gain, (ROW_BLOCK, width))
    aff_ref[1] = jnp.broadcast_to(mod_ref[0, 3 * sub:3 * sub + 1, :], (ROW_BLOCK, width))

    def body(rows):
        y = x_ref[rows, :] * _lanes(stats_ref[rows, :], width) * aff_ref[0] + aff_ref[1]
        xn_ref[rows, :] = y.astype(BF16)

    _row_blocks(n_rows, body)


def _mod_kernel(c_ref, w_ref, b_ref, o_ref):
    cs = _silu(c_ref[...]).astype(BF16)
    o_ref[0] = _dot(cs, w_ref[0].astype(BF16)) + b_ref[0]


def _modulation(cond, mod_w, mod_b):
    depth, d, n = mod_w.shape
    rows = cond.shape[0]
    return pl.pallas_call(
        _mod_kernel,
        grid=(depth, n // TN),
        in_specs=[pl.BlockSpec((rows, d), lambda l, j: (0, 0)),
                  pl.BlockSpec((1, d, TN), lambda l, j: (l, 0, j)),
                  pl.BlockSpec((1, 1, TN), lambda l, j: (l, 0, j))],
        out_specs=pl.BlockSpec((1, rows, TN), lambda l, j: (l, 0, j)),
        out_shape=jax.ShapeDtypeStruct((depth, rows, n), F32),
        compiler_params=_cp(("arbitrary", "arbitrary")),
        name="adaln_mod",
    )(cond, mod_w, mod_b.reshape(depth, 1, n))


def _ffn_kernel(x_ref, mod_ref, g_ref, wg_ref, wu_ref, wd_ref, fg_ref, o_ref, xn_ref, acc_ref, stats_ref, aff_ref,
                *, sub, final):
    f = pl.program_id(1)
    n_rows, width = x_ref.shape

    @pl.when(f == 0)
    def _():
        _prenorm_into(x_ref, xn_ref, g_ref, mod_ref, sub, stats_ref, aff_ref)
        acc_ref[...] = jnp.zeros_like(acc_ref)

    xn = xn_ref[...]
    h = (_silu(_dot(xn, wg_ref[...])) * _dot(xn, wu_ref[...])).astype(BF16)
    acc_ref[...] += _dot(h, wd_ref[...])

    @pl.when(f == pl.num_programs(1) - 1)
    def _():
        def body(rows):
            o_ref[rows, :] = x_ref[rows, :] + 0.5 * mod_ref[0, 3 * sub + 2:3 * sub + 3, :] * acc_ref[rows, :]

        _row_blocks(n_rows, body)
        if final:
            _rms_scale_into(lambda rows: o_ref[rows, :], stats_ref, n_rows, width)

            def norm(rows):
                o_ref[rows, :] = o_ref[rows, :] * _lanes(stats_ref[rows, :], width) * fg_ref[...]

            _row_blocks(n_rows, norm)


def _ffn(x, mods, mod_of_tile, norm_g, wg, wu, wd, layer, half, final_g, final, tm):
    n_tok, d = x.shape
    ff = wg.shape[3]
    sub = 2 * half
    return pl.pallas_call(
        functools.partial(_ffn_kernel, sub=sub, final=final),
        grid=(n_tok // tm, ff // TF),
        in_specs=[pl.BlockSpec((tm, d), lambda i, f: (i, 0)),
                  pl.BlockSpec((1, N_MOD, d), lambda i, f: (mod_of_tile(i), 0, 0)),
                  pl.BlockSpec((1, d), lambda i, f: (0, 0)),
                  pl.BlockSpec((None, None, d, TF), lambda i, f: (layer, half, 0, f)),
                  pl.BlockSpec((None, None, d, TF), lambda i, f: (layer, half, 0, f)),
                  pl.BlockSpec((None, None, TF, d), lambda i, f: (layer, half, f, 0)),
                  pl.BlockSpec((1, d), lambda i, f: (0, 0))],
        out_specs=pl.BlockSpec((tm, d), lambda i, f: (i, 0)),
        out_shape=jax.ShapeDtypeStruct((n_tok, d), F32),
        scratch_shapes=[pltpu.VMEM((tm, d), BF16), pltpu.VMEM((tm, d), F32), pltpu.VMEM((tm, V7X_LANES), F32),
                        pltpu.VMEM((2, ROW_BLOCK, d), F32)],
        compiler_params=_cp(("parallel", "arbitrary")),
        name="swiglu",
    )(x, mods, norm_g.reshape(1, d), wg, wu, wd, final_g.reshape(1, d))


def _inproj_even_kernel(x_ref, mod_ref, g_ref, w_ref, u_ref, qkv_ref, xn_ref, stats_ref, aff_ref, *, s5w):
    _prenorm_into(x_ref, xn_ref, g_ref, mod_ref, 1, stats_ref, aff_ref)
    xn = xn_ref[...]
    u_ref[...] = _dot(xn, w_ref[:, :s5w])
    qkv_ref[...] = _dot(xn, w_ref[:, s5w:]).astype(BF16)


def _inproj_even(x, mods, mod_of_tile, norm_g, w, s5w, seq, nb, tm):
    n_tok, d = x.shape
    n_in = w.shape[1]
    tps = seq // tm
    return pl.pallas_call(
        functools.partial(_inproj_even_kernel, s5w=s5w),
        grid=(n_tok // tm,),
        in_specs=[pl.BlockSpec((tm, d), lambda i: (i, 0)),
                  pl.BlockSpec((1, N_MOD, d), lambda i: (mod_of_tile(i), 0, 0)),
                  pl.BlockSpec((1, d), lambda i: (0, 0)),
                  pl.BlockSpec((d, n_in), lambda i: (0, 0), pipeline_mode=pl.Buffered(1))],
        out_specs=[pl.BlockSpec((tm, s5w), lambda i: (i % tps, i // tps)),
                   pl.BlockSpec((tm, n_in - s5w), lambda i: (i, 0))],
        out_shape=[jax.ShapeDtypeStruct((seq, nb * s5w), F32),
                   jax.ShapeDtypeStruct((n_tok, n_in - s5w), BF16)],
        scratch_shapes=[pltpu.VMEM((tm, d), BF16), pltpu.VMEM((tm, V7X_LANES), F32), pltpu.VMEM((2, ROW_BLOCK, d), F32)],
        compiler_params=_cp(("parallel",)),
        name="inproj_even",
    )(x, mods, norm_g.reshape(1, d), w)


def _inproj_odd_kernel(x_ref, mod_ref, g_ref, w_ref, wdt_ref, p_ref, dt_ref, xn_ref, stats_ref, aff_ref):
    n = pl.program_id(1)

    @pl.when(n == 0)
    def _():
        _prenorm_into(x_ref, xn_ref, g_ref, mod_ref, 1, stats_ref, aff_ref)
        dt_ref[...] = _dot(xn_ref[...], wdt_ref[...])

    p_ref[...] = _dot(xn_ref[...], w_ref[...])


def _inproj_odd(x, mods, mod_of_tile, norm_g, w, wdt, tm):
    n_tok, d = x.shape
    n_in = w.shape[1]
    n_dt = wdt.shape[1]
    return pl.pallas_call(
        _inproj_odd_kernel,
        grid=(n_tok // tm, n_in // TN),
        in_specs=[pl.BlockSpec((tm, d), lambda i, n: (i, 0)),
                  pl.BlockSpec((1, N_MOD, d), lambda i, n: (mod_of_tile(i), 0, 0)),
                  pl.BlockSpec((1, d), lambda i, n: (0, 0)),
                  pl.BlockSpec((d, TN), lambda i, n: (0, n)),
                  pl.BlockSpec((d, n_dt), lambda i, n: (0, 0))],
        out_specs=[pl.BlockSpec((tm, TN), lambda i, n: (i, n)),
                   pl.BlockSpec((tm, n_dt), lambda i, n: (i, 0))],
        out_shape=[jax.ShapeDtypeStruct((n_tok, n_in), F32),
                   jax.ShapeDtypeStruct((n_tok, n_dt), F32)],
        scratch_shapes=[pltpu.VMEM((tm, d), BF16), pltpu.VMEM((tm, V7X_LANES), F32), pltpu.VMEM((2, ROW_BLOCK, d), F32)],
        compiler_params=_cp(("parallel", "arbitrary")),
        name="inproj_odd",
    )(x, mods, norm_g.reshape(1, d), w, wdt)


def _outproj_even_kernel(a_ref, b_ref, x_ref, mod_ref, wa_ref, wb_ref, o_ref):
    y = _dot(a_ref[...], wa_ref[...]) + _dot(b_ref[...], wb_ref[...])
    o_ref[...] = x_ref[...] + mod_ref[0, 5:6, :] * y


def _outproj_even(a, b, x, mods, mod_of_tile, w, tm):
    n_tok, d = x.shape
    half = a.shape[1]
    return pl.pallas_call(
        _outproj_even_kernel,
        grid=(n_tok // tm,),
        in_specs=[pl.BlockSpec((tm, half), lambda i: (i, 0)),
                  pl.BlockSpec((tm, half), lambda i: (i, 0)),
                  pl.BlockSpec((tm, d), lambda i: (i, 0)),
                  pl.BlockSpec((1, N_MOD, d), lambda i: (mod_of_tile(i), 0, 0)),
                  pl.BlockSpec((half, d), lambda i: (0, 0)),
                  pl.BlockSpec((half, d), lambda i: (1, 0))],
        out_specs=pl.BlockSpec((tm, d), lambda i: (i, 0)),
        out_shape=jax.ShapeDtypeStruct((n_tok, d), F32),
        compiler_params=_cp(("parallel",)),
        name="outproj_even",
    )(a, b, x, mods, w, w)


def _outproj_odd_kernel(hy_ref, ys_ref, z_ref, ng_ref, x_ref, mod_ref, wa_ref, wb_ref, o_ref, b_ref, y_ref, stats_ref):
    n_rows, width = ys_ref.shape

    def gated(rows):
        y_ref[rows, :] = ys_ref[rows, :] * _silu(z_ref[rows, :])

    _row_blocks(n_rows, gated)
    _rms_scale_into(lambda rows: y_ref[rows, :], stats_ref, n_rows, width)

    def body(rows):
        b_ref[rows, :] = (y_ref[rows, :] * _lanes(stats_ref[rows, :], width) * ng_ref[...]).astype(BF16)

    _row_blocks(n_rows, body)
    y = _dot(hy_ref[...].astype(BF16), wa_ref[...]) + _dot(b_ref[...], wb_ref[...])
    o_ref[...] = x_ref[...] + mod_ref[0, 5:6, :] * y


def _outproj_odd(hy, ys, p, z_block, norm_g, x, mods, mod_of_tile, w, tm):
    n_tok, d = x.shape
    half = hy.shape[1]
    return pl.pallas_call(
        _outproj_odd_kernel,
        grid=(n_tok // tm,),
        in_specs=[pl.BlockSpec((tm, half), lambda i: (i, 0)),
                  pl.BlockSpec((tm, half), lambda i: (i, 0)),
                  pl.BlockSpec((tm, half), lambda i: (i, z_block)),
                  pl.BlockSpec((1, half), lambda i: (0, 0)),
                  pl.BlockSpec((tm, d), lambda i: (i, 0)),
                  pl.BlockSpec((1, N_MOD, d), lambda i: (mod_of_tile(i), 0, 0)),
                  pl.BlockSpec((half, d), lambda i: (0, 0)),
                  pl.BlockSpec((half, d), lambda i: (1, 0))],
        out_specs=pl.BlockSpec((tm, d), lambda i: (i, 0)),
        out_shape=jax.ShapeDtypeStruct((n_tok, d), F32),
        scratch_shapes=[pltpu.VMEM((tm, half), BF16), pltpu.VMEM((tm, half), F32), pltpu.VMEM((tm, V7X_LANES), F32)],
        compiler_params=_cp(("parallel",)),
        name="outproj_odd",
    )(hy, ys, p, norm_g.reshape(1, half), x, mods, w, w)


def _s5_kernel(uc_ref, ul_ref, a_ref, bre_ref, bim_ref, cre_ref, cim_ref, yc_ref, yl_ref,
               s_ref, carry_ref, ub_ref, yb_ref, *, nc_c, tc, nb, kt_n, half):
    d = pl.program_id(0)
    j = pl.program_id(1)
    kin = bre_ref.shape[2]
    kst = bre_ref.shape[3]
    ln = ub_ref.shape[2]
    w = ub_ref.shape[0] * ln
    lpk = kin // ln

    @pl.when(j == 0)
    def _():
        carry_ref[...] = jnp.zeros_like(carry_ref)

    def gather(u_ref):
        for b in range(nb):
            for cb in range(w // ln):
                ub_ref[cb, pl.ds(b, tc, stride=nb), :] = u_ref[:, b * w + cb * ln:b * w + (cb + 1) * ln]

    def scatter(y_ref):
        for b in range(nb):
            for cb in range(w // ln):
                y_ref[0, :, b * w + cb * ln:b * w + (cb + 1) * ln] = yb_ref[cb, pl.ds(b, tc, stride=nb), :]

    def run(rev):
        for kt in range(kt_n):
            uk = jnp.concatenate([ub_ref[kt * lpk + i] for i in range(lpk)], axis=1).astype(BF16)
            s_ref[:, kt * kst:(kt + 1) * kst] = _dot(uk, bre_ref[0, kt])
            s_ref[:, half + kt * kst:half + (kt + 1) * kst] = _dot(uk, bim_ref[0, kt])
            for c0 in range(kt * kst, (kt + 1) * kst, S5_W):
                a_re = jnp.broadcast_to(a_ref[0, 0:1, c0:c0 + S5_W], (nb, S5_W))
                a_im = jnp.broadcast_to(a_ref[0, 1:2, c0:c0 + S5_W], (nb, S5_W))
                s_re = carry_ref[:, c0:c0 + S5_W]
                s_im = carry_ref[:, half + c0:half + c0 + S5_W]
                for i in range(tc):
                    row = (tc - 1 - i if rev else i) * nb
                    n_re = a_re * s_re - a_im * s_im + s_ref[row:row + nb, c0:c0 + S5_W]
                    n_im = a_re * s_im + a_im * s_re + s_ref[row:row + nb, half + c0:half + c0 + S5_W]
                    s_ref[row:row + nb, c0:c0 + S5_W] = n_re
                    s_ref[row:row + nb, half + c0:half + c0 + S5_W] = n_im
                    s_re, s_im = n_re, n_im
                carry_ref[:, c0:c0 + S5_W] = s_re
                carry_ref[:, half + c0:half + c0 + S5_W] = s_im
            p_re = s_ref[:, kt * kst:(kt + 1) * kst].astype(BF16)
            p_im = s_ref[:, half + kt * kst:half + (kt + 1) * kst].astype(BF16)
            yk = _dot(p_re, cre_ref[0, kt]) + _dot(p_im, cim_ref[0, kt])
            for i in range(lpk):
                yb_ref[kt * lpk + i] = yk[:, i * ln:(i + 1) * ln]

    @pl.when(j < nc_c)
    def _():
        gather(uc_ref)

    @pl.when(j >= nc_c)
    def _():
        gather(ul_ref)

    @pl.when(d == 0)
    def _():
        run(False)

    @pl.when(d == 1)
    def _():
        run(True)

    @pl.when(j < nc_c)
    def _():
        scatter(yc_ref)

    @pl.when(j >= nc_c)
    def _():
        scatter(yl_ref)


def _s5_scan(uc, ul, a_bar, bre, bim, cre, cim, nb):
    lc, wide = uc.shape
    seq = ul.shape[0]
    w = wide // nb
    r = S5_TC * nb
    nc_c, nc_l = lc // S5_TC, seq // S5_TC
    kt_n = bre.shape[1]
    half = a_bar.shape[2]

    def c_idx(d, j):
        jc = jnp.minimum(j, nc_c - 1)
        return jnp.where(d == 0, jc, nc_c - 1 - jc)

    def l_idx(d, j):
        jl = jnp.maximum(j - nc_c, 0)
        return jnp.where(d == 0, jl, nc_l - 1 - jl)

    wspec = lambda shp: pl.BlockSpec((1,) + shp, lambda d, j: (d,) + (0,) * len(shp))
    return pl.pallas_call(
        functools.partial(_s5_kernel, nc_c=nc_c, tc=S5_TC, nb=nb, kt_n=kt_n, half=half),
        grid=(2, nc_c + nc_l),
        in_specs=[pl.BlockSpec((S5_TC, wide), lambda d, j: (c_idx(d, j), 0)),
                  pl.BlockSpec((S5_TC, wide), lambda d, j: (l_idx(d, j), 0)),
                  wspec(a_bar.shape[1:]), wspec(bre.shape[1:]), wspec(bim.shape[1:]),
                  wspec(cre.shape[1:]), wspec(cim.shape[1:])],
        out_specs=[pl.BlockSpec((1, S5_TC, wide), lambda d, j: (d, c_idx(d, j), 0)),
                   pl.BlockSpec((1, S5_TC, wide), lambda d, j: (d, l_idx(d, j), 0))],
        out_shape=[jax.ShapeDtypeStruct((2, lc, wide), F32), jax.ShapeDtypeStruct((2, seq, wide), F32)],
        scratch_shapes=[pltpu.VMEM((r, 2 * half), F32), pltpu.VMEM((nb, 2 * half), F32),
                        pltpu.VMEM((w // V7X_LANES, r, V7X_LANES), F32),
                        pltpu.VMEM((w // V7X_LANES, r, V7X_LANES), F32)],
        compiler_params=_cp(("arbitrary", "arbitrary")),
        name="s5_scan",
    )(uc, ul, a_bar, bre, bim, cre, cim)


def _s5_params(a_re, a_im, log_dt, b_re, b_im, c_re, c_im):
    l_re, l_im = a_re.astype(F32), a_im.astype(F32)
    dt = jnp.exp(log_dt.astype(F32))[..., None]
    mag = jnp.exp(l_re * dt)
    ab_re, ab_im = mag * jnp.cos(l_im * dt), mag * jnp.sin(l_im * dt)
    inv = 1.0 / (l_re * l_re + l_im * l_im)
    q_re = ((ab_re - 1.0) * l_re + ab_im * l_im) * inv
    q_im = (ab_im * l_re - (ab_re - 1.0) * l_im) * inv
    bb_re = q_re[..., None] * b_re.astype(F32) - q_im[..., None] * b_im.astype(F32)
    bb_im = q_re[..., None] * b_im.astype(F32) + q_im[..., None] * b_re.astype(F32)
    groups, p_dim, h_dim = bb_re.shape[1:]
    gl = V7X_LANES * 2 // h_dim
    kt = groups // gl
    eye = jnp.eye(gl, dtype=F32)
    a_pack = jnp.stack([ab_re.reshape(2, groups * p_dim), ab_im.reshape(2, groups * p_dim)], axis=1)

    def pack_b(x):
        x = x.reshape(2, kt, gl, p_dim, h_dim)
        return jnp.einsum('lm,dklph->dklhmp', eye, x).reshape(2, kt, gl * h_dim, gl * p_dim).astype(BF16)

    def pack_c(x):
        x = x.reshape(2, kt, gl, h_dim, p_dim)
        return jnp.einsum('lm,dklhp->dklpmh', eye, x).reshape(2, kt, gl * p_dim, gl * h_dim).astype(BF16)

    return (a_pack, pack_b(bb_re), pack_b(bb_im),
            pack_c(c_re.astype(F32)), pack_c(-c_im.astype(F32)))


def _s5_glu_kernel(u_ref, y_ref, d_ref, w_ref, b_ref, o_ref):
    y = d_ref[...] * u_ref[...] + y_ref[0] + y_ref[1]
    g = 0.5 * y * (1.0 + jnp.tanh(math.sqrt(2.0 / math.pi) * (y + 0.044715 * (y * y * y))))
    o_ref[...] = (g * _sigmoid(_dot(g.astype(BF16), w_ref[...]) + b_ref[...])).astype(BF16)


def _s5_glu(u_tm, y_tm, d_skip, glu_w, glu_b, seq, nb, tm):
    w = glu_w.shape[0]
    tps = seq // tm
    return pl.pallas_call(
        _s5_glu_kernel,
        grid=(nb, tps),
        in_specs=[pl.BlockSpec((tm, w), lambda b, i: (i, b)),
                  pl.BlockSpec((2, tm, w), lambda b, i: (0, i, b)),
                  pl.BlockSpec((1, w), lambda b, i: (0, 0)),
                  pl.BlockSpec((w, w), lambda b, i: (0, 0)),
                  pl.BlockSpec((1, w), lambda b, i: (0, 0))],
        out_specs=pl.BlockSpec((tm, w), lambda b, i: (b * tps + i, 0)),
        out_shape=jax.ShapeDtypeStruct((nb * seq, w), BF16),
        compiler_params=_cp(("parallel", "arbitrary")),
        name="s5_glu",
    )(u_tm, y_tm, d_skip.reshape(1, w), glu_w, glu_b.reshape(1, w))


def _na_groups(rows):
    wr = NA_GQ + NA_KH - 1
    plan, sigs = [], {}
    for g in range(rows // NA_GQ):
        r0 = g * NA_GQ
        ws = min(max(r0 - NA_KH // 2, 0), rows - wr)
        sig = tuple(min(max(r - NA_KH // 2, 0), rows - NA_KH) - ws for r in range(r0, r0 + NA_GQ)) + (r0 - ws,)
        tid = sigs.setdefault(sig, len(sigs))
        plan.append((r0, ws, tid))
    reps = [next(p for p in plan if p[2] == t) for t in range(len(sigs))]
    return plan, reps, wr


def _na_bias(rpb, rows):
    _, reps, wr = _na_groups(rows)
    col = np.arange(GRID_W)
    cs = np.clip(col - NA_KW // 2, 0, GRID_W - NA_KW)
    col_ok = (col[None, :] >= cs[:, None]) & (col[None, :] < cs[:, None] + NA_KW)
    dc = np.clip(col[None, :] - col[:, None] + NA_KW - 1, 0, 2 * NA_KW - 2)
    pick = (dc[None] == np.arange(2 * NA_KW - 1)[:, None, None]).astype(np.float32)
    tcol = jnp.einsum('hdc,cqk->hdqk', rpb.astype(F32), pick, precision=lax.Precision.HIGHEST)
    tcol = jnp.where(col_ok[None, None], tcol, NEG)
    dead = jnp.full(tcol[:, 0].shape, NEG, F32)
    tabs = []
    for r0, ws, _ in reps:
        row_blocks = []
        for r in range(r0, r0 + NA_GQ):
            rs = min(max(r - NA_KH // 2, 0), rows - NA_KH)
            blocks = [tcol[:, kr - r + NA_KH - 1] if rs <= kr < rs + NA_KH else dead for kr in range(ws, ws + wr)]
            row_blocks.append(jnp.concatenate(blocks, axis=-1))
        tabs.append(jnp.concatenate(row_blocks, axis=-2))
    return jnp.stack(tabs)


def _na_kernel(q_ref, k_ref, v_ref, qc_ref, kc_ref, vc_ref, bias_ref, o_ref, oc_ref, *, plan, wr, scale):
    kc = kc_ref[0]
    vc = vc_ref[0]
    nq = NA_GQ * GRID_W
    nk = wr * GRID_W
    for r0, ws, tid in plan:
        q = q_ref[0, r0 * GRID_W:r0 * GRID_W + nq, :]
        kw = k_ref[0, ws * GRID_W:ws * GRID_W + nk, :]
        vw = v_ref[0, ws * GRID_W:ws * GRID_W + nk, :]
        s1 = _dot_nt(q, kw) * scale + bias_ref[tid, 0]
        s2 = _dot_nt(q, kc) * scale
        m = jnp.maximum(jnp.max(s1, axis=-1, keepdims=True), jnp.max(s2, axis=-1, keepdims=True))
        p1 = jnp.exp(s1 - m)
        p2 = jnp.exp(s2 - m)
        den = jnp.sum(p1, axis=-1, keepdims=True) + jnp.sum(p2, axis=-1, keepdims=True)
        o = (_dot(p1.astype(BF16), vw) + _dot(p2.astype(BF16), vc)) / den
        o_ref[0, r0 * GRID_W:r0 * GRID_W + nq, :] = o.astype(BF16)
    s = _dot_nt(qc_ref[0], kc) * scale
    p = jnp.exp(s - jnp.max(s, axis=-1, keepdims=True))
    oc = _dot(p.astype(BF16), vc) / jnp.sum(p, axis=-1, keepdims=True)
    oc_ref[0] = oc.astype(BF16)


def _na_attention(qkv_l, qkv_c, bias, heads):
    bsz, seq, _ = qkv_l.shape
    lc = qkv_c.shape[1]
    hd = NA_HEAD_DIM
    plan, _, wr = _na_groups(seq // GRID_W)
    n_types = bias.shape[0]
    spec = lambda t, off: pl.BlockSpec((1, t, hd), lambda b, h: (b, 0, off + h))
    return pl.pallas_call(
        functools.partial(_na_kernel, plan=plan, wr=wr, scale=hd ** -0.5),
        grid=(bsz, heads),
        in_specs=[spec(seq, 0), spec(seq, heads), spec(seq, 2 * heads),
                  spec(lc, 0), spec(lc, heads), spec(lc, 2 * heads),
                  pl.BlockSpec((n_types, 1) + bias.shape[2:], lambda b, h: (0, h, 0, 0))],
        out_specs=[pl.BlockSpec((1, seq, hd), lambda b, h: (b, 0, h)),
                   pl.BlockSpec((1, lc, hd), lambda b, h: (b, 0, h))],
        out_shape=[jax.ShapeDtypeStruct((bsz, seq, heads * hd), BF16),
                   jax.ShapeDtypeStruct((bsz, lc, heads * hd), BF16)],
        compiler_params=_cp(("parallel", "arbitrary")),
        name="nbr_attention",
    )(qkv_l, qkv_l, qkv_l, qkv_c, qkv_c, qkv_c, bias)


def _dwconv3(load, t_len, r0, rows, w, b):
    main = load(r0, rows)
    idx = lax.broadcasted_iota(jnp.int32, main.shape, 0)
    prev = load(r0 - 1, 1) if r0 > 0 else jnp.zeros_like(main[0:1])
    nxt = load(r0 + rows, 1) if r0 + rows < t_len else jnp.zeros_like(main[0:1])
    up = jnp.where(idx == 0, prev, pltpu.roll(main, 1, 0))
    down = jnp.where(idx == rows - 1, nxt, pltpu.roll(main, rows - 1, 0))
    return w[0:1] * up + w[1:2] * main + w[2:3] * down + b


CONV_ROWS = 256


def _ssd_kernel(xc_ref, bc_ref, cc_ref, dtc_ref, xl_ref, bl_ref, cl_ref, dtl_ref,
                wx_ref, wb_ref, wc_ref, bx_ref, bb_ref, bcb_ref, dtb_ref, a_ref, dsk_ref,
                y_ref, xs_ref, bs_ref, cs_ref, dts_ref, cum_ref, st_ref, dec_ref, *, lc, seq, e_n):
    q = SSD_CHUNK
    p = SSD_HEAD_DIM
    ncc, ncl = lc // q, seq // q
    nch = ncc + ncl

    for src, dst, w_ref, b_ref in ((0, xs_ref, wx_ref, bx_ref), (1, bs_ref, wb_ref, bb_ref), (2, cs_ref, wc_ref, bcb_ref)):
        w = w_ref[...]
        b = b_ref[...]
        for base, t_len, ref in ((0, lc, (xc_ref, bc_ref, cc_ref)[src]), (lc, seq, (xl_ref, bl_ref, cl_ref)[src])):
            cr = min(CONV_ROWS, t_len)
            for r0 in range(0, t_len, cr):
                dst[base + r0:base + r0 + cr, :] = _silu(
                    _dwconv3(lambda a, n, ref=ref: ref[0, a:a + n, :], t_len, r0, cr, w, b))
    for base, t_len, ref in ((0, lc, dtc_ref), (lc, seq, dtl_ref)):
        v = ref[0] + dtb_ref[...]
        dts_ref[base:base + t_len, :] = jnp.maximum(v, 0.0) + jnp.log(1.0 + jnp.exp(-jnp.abs(v)))

    ii = lax.broadcasted_iota(jnp.int32, (q, q), 0)
    jj = lax.broadcasted_iota(jnp.int32, (q, q), 1)
    keep = (jj <= ii, jj >= ii)
    tri = tuple(jnp.where(m, 1.0, 0.0).astype(BF16) for m in keep)
    a_row = a_ref[...]
    dsk = dsk_ref[...]
    xw = e_n * p
    fwd_lane = lax.broadcasted_iota(jnp.int32, (q, V7X_LANES), 1) < e_n
    spread = jnp.where(lax.broadcasted_iota(jnp.int32, (V7X_LANES, 2 * xw), 0)
                       == lax.broadcasted_iota(jnp.int32, (V7X_LANES, 2 * xw), 1) // p, 1.0, 0.0).astype(BF16)

    def expand(v):
        hi, lo = _split2(v)
        return _dot(hi, spread) + _dot(lo, spread)

    def local(c, carry):
        r0 = pl.multiple_of(c * q, q)
        x = xs_ref[pl.ds(r0, q), :]
        dt = dts_ref[pl.ds(r0, q), :]
        dta = dt * a_row
        d_hi = dta.astype(BF16)
        r1 = dta - d_hi.astype(F32)
        d_mid = r1.astype(BF16)
        d_lo = (r1 - d_mid.astype(F32)).astype(BF16)
        bm_t = bs_ref[pl.ds(r0, q), :].T.astype(BF16)
        cum = [_dot(tri[k], d_hi) + (_dot(tri[k], d_mid) + _dot(tri[k], d_lo)) for k in range(2)]
        cum = jnp.where(fwd_lane, cum[0], cum[1])
        cum_ref[pl.ds(r0, q), :] = cum
        tot = jnp.where(fwd_lane[0:1], cum[q - 1:q, :], cum[0:1, :])
        wgt = expand(dt * jnp.exp(tot - cum))
        st = _dot(bm_t, (jnp.concatenate([x, x], axis=1) * wgt).astype(BF16))
        dec = expand(jnp.broadcast_to(jnp.exp(tot), (V7X_SUBLANES, V7X_LANES)))[0:1]
        for k in range(2):
            st_ref[k, c] = st[:, k * xw:(k + 1) * xw]
            dec_ref[k, c] = dec[:, k * xw:(k + 1) * xw]
        return carry

    lax.fori_loop(0, nch, local, 0, unroll=2)

    for k in range(2):
        def prop(i, s, k=k):
            c = i if k == 0 else jnp.where(i < ncc, ncc - 1 - i, nch - 1 - (i - ncc))
            loc = st_ref[k, c]
            st_ref[k, c] = s
            return s * dec_ref[k, c] + loc

        lax.fori_loop(0, nch, prop, jnp.zeros(st_ref.shape[2:], F32))

    def emit(i, carry):
        c = ncc + i
        r0 = pl.multiple_of(c * q, q)
        x = xs_ref[pl.ds(r0, q), :]
        dt = dts_ref[pl.ds(r0, q), :]
        cm = cs_ref[pl.ds(r0, q), :].astype(BF16)
        gmat = _dot_nt(cm, bs_ref[pl.ds(r0, q), :].astype(BF16))
        cum = cum_ref[pl.ds(r0, q), :]
        cum_t = cum.T
        xdt = (jnp.concatenate([x, x], axis=1) * expand(dt)).astype(BF16)
        ydiag = []
        for e in range(e_n):
            lhs, rhs = [], []
            for k in range(2):
                col = k * e_n + e
                seg = jnp.where(keep[k], cum[:, col:col + 1] - cum_t[col:col + 1, :], NEG)
                lhs.append((gmat * jnp.exp(seg)).astype(BF16))
                rhs.append(xdt[:, k * xw + e * p:k * xw + (e + 1) * p])
            ydiag.append(_dot(jnp.concatenate(lhs, axis=1), jnp.concatenate(rhs, axis=0)))
        y = jnp.concatenate(ydiag, axis=1) + dsk * x
        eoff = expand(jnp.exp(cum))
        for k in range(2):
            y = y + _dot(cm, st_ref[k, c].astype(BF16)) * eoff[:, k * xw:(k + 1) * xw]
        y_ref[0, pl.ds(pl.multiple_of(i * q, q), q), :] = y
        return carry

    lax.fori_loop(0, ncl, emit, 0, unroll=4)


def _ssd(p_l, dt_l, p_c, dt_c, x_off_l, x_off_c, conv_w, conv_b, dtb, a_neg, d_exp, inner):
    bsz, seq, _ = p_l.shape
    lc = p_c.shape[1]
    e_n = inner // SSD_HEAD_DIM // SSD_GROUPS
    xw = e_n * SSD_HEAD_DIM
    n = SSD_STATE
    t_tot = lc + seq

    def specs(t, off):
        return [pl.BlockSpec((1, t, xw), lambda b, g: (b, 0, off // xw + g)),
                pl.BlockSpec((1, t, n), lambda b, g: (b, 0, (off + inner) // n + g)),
                pl.BlockSpec((1, t, n), lambda b, g: (b, 0, (off + inner) // n + SSD_GROUPS + g)),
                pl.BlockSpec((1, t, V7X_LANES), lambda b, g: (b, 0, g))]

    def pspecs(rows):
        return [pl.BlockSpec((rows, xw), lambda b, g: (0, g)),
                pl.BlockSpec((rows, n), lambda b, g: (0, inner // n + g)),
                pl.BlockSpec((rows, n), lambda b, g: (0, inner // n + SSD_GROUPS + g))]

    lane = lambda: pl.BlockSpec((1, V7X_LANES), lambda b, g: (0, g))
    return pl.pallas_call(
        functools.partial(_ssd_kernel, lc=lc, seq=seq, e_n=e_n),
        grid=(bsz, SSD_GROUPS),
        in_specs=specs(lc, x_off_c) + specs(seq, x_off_l) + pspecs(3) + pspecs(1)
        + [lane(), lane(), pl.BlockSpec((1, xw), lambda b, g: (0, g))],
        out_specs=pl.BlockSpec((1, seq, xw), lambda b, g: (b, 0, g)),
        out_shape=jax.ShapeDtypeStruct((bsz, seq, inner), F32),
        scratch_shapes=[pltpu.VMEM((t_tot, xw), F32), pltpu.VMEM((t_tot, n), F32), pltpu.VMEM((t_tot, n), F32),
                        pltpu.VMEM((t_tot, V7X_LANES), F32), pltpu.VMEM((t_tot, V7X_LANES), F32),
                        pltpu.VMEM((2, t_tot // SSD_CHUNK, n, xw), F32), pltpu.VMEM((2, t_tot // SSD_CHUNK, 1, xw), F32)],
        compiler_params=_cp(("parallel", "arbitrary")),
        name="ssd",
    )(p_c, p_c, p_c, dt_c, p_l, p_l, p_l, dt_l, conv_w, conv_w, conv_w,
      conv_b, conv_b, conv_b, dtb, a_neg, d_exp)


def _hy_filter_kernel(z_ref, w_in_ref, b_in_ref, w_mid_ref, b_mid_ref, w_out_ref, freq_ref, delta_ref, o_ref, h_ref):
    j = pl.program_id(1)
    hp = lax.Precision.HIGHEST

    @pl.when(j == 0)
    def _():
        freq = freq_ref[...]
        h = jnp.sin(freq * (jnp.dot(z_ref[...], w_in_ref[...], precision=hp, preferred_element_type=F32) + b_in_ref[...]))
        for i in range(w_mid_ref.shape[0]):
            h = jnp.sin(freq * (jnp.dot(h, w_mid_ref[i], precision=hp, preferred_element_type=F32) + b_mid_ref[i]))
        h_ref[...] = h

    decay = jnp.exp(-z_ref[:, 0:1] * delta_ref[...])
    o_ref[...] = jnp.dot(h_ref[...], w_out_ref[...], precision=hp, preferred_element_type=F32) * decay


def _hy_filters(seq, w_in, b_in, w_mid, b_mid, w_out, freq, width):
    t = jnp.linspace(0.0, 1.0, seq, dtype=F32)[:, None]
    w = 2.0 * math.pi * jnp.arange(seq, dtype=F32)[:, None] / seq
    f = jnp.linspace(1e-4, HY_BANDS - 1, HY_BANDS, dtype=F32)[None, :]
    z = jnp.concatenate([t, jnp.cos(f * w), -jnp.sin(f * w)], axis=-1)
    emb, hid = w_in.shape
    pad = V7X_LANES
    z = jnp.pad(z, ((0, 0), (0, pad - emb)))
    w_in_p = jnp.pad(w_in.astype(F32), ((0, pad - emb), (0, pad - hid)))
    padv = lambda v: jnp.pad(v.astype(F32), ((0, 0), (0, pad - hid)))
    w_mid_p = jnp.pad(w_mid.astype(F32), ((0, 0), (0, pad - hid), (0, pad - hid)))
    w_out_p = jnp.pad(w_out.astype(F32), ((0, pad - hid), (0, 0)))
    n_out = w_out.shape[1]
    max_decay = math.log(HY_DECAY_TARGET) / HY_FAST_PCT
    min_decay = math.log(HY_DECAY_TARGET) / HY_SLOW_PCT
    deltas = jnp.abs(jnp.linspace(min_decay, max_decay, width, dtype=F32))[None, :]
    tr = 256
    n_mid = w_mid.shape[0]
    return pl.pallas_call(
        _hy_filter_kernel,
        grid=(seq // tr, n_out // width),
        in_specs=[pl.BlockSpec((tr, pad), lambda i, j: (i, 0)),
                  pl.BlockSpec((pad, pad), lambda i, j: (0, 0)),
                  pl.BlockSpec((1, pad), lambda i, j: (0, 0)),
                  pl.BlockSpec((n_mid, pad, pad), lambda i, j: (0, 0, 0)),
                  pl.BlockSpec((n_mid, 1, pad), lambda i, j: (0, 0, 0)),
                  pl.BlockSpec((pad, width), lambda i, j: (0, j)),
                  pl.BlockSpec((1, pad), lambda i, j: (0, 0)),
                  pl.BlockSpec((1, width), lambda i, j: (0, 0))],
        out_specs=pl.BlockSpec((tr, width), lambda i, j: (i, j)),
        out_shape=jax.ShapeDtypeStruct((seq, n_out), F32),
        scratch_shapes=[pltpu.VMEM((tr, pad), F32)],
        compiler_params=_cp(("parallel", "arbitrary")),
        name="hyena_filter",
    )(z, w_in_p, padv(b_in[None]), w_mid_p, padv(b_mid)[:, None, :], w_out_p, padv(freq[None]), deltas)


def _dft_tables(seq):
    n = 2 * seq
    k = jnp.arange(seq, dtype=jnp.int32)
    blk = DFT_BLOCK
    th = 2.0 * math.pi / n
    ang_a = th * ((blk * k[:seq // blk, None] * k[None, :]) % n).astype(F32)
    ang_b = th * ((k[:blk, None] * k[None, :]) % n).astype(F32)
    ca, sa = jnp.cos(ang_a)[:, None, :], jnp.sin(ang_a)[:, None, :]
    cb, sb = jnp.cos(ang_b)[None], jnp.sin(ang_b)[None]
    cos = (ca * cb - sa * sb).reshape(seq, seq)
    sin = (sa * cb + ca * sb).reshape(seq, seq)
    alt = jnp.where(k % 2 == 0, 1.0, -1.0).astype(F32)
    sf = jnp.where(k[:, None] == 0, alt[None, :], sin)
    wk = jnp.where(k == 0, 1.0, 2.0).astype(F32) / n
    ci = cos * wk[None, :]
    si = jnp.where(k[None, :] == 0, alt[:, None] / n, sin * (2.0 / n))
    return _split2(cos) + _split2(sf) + (ci.astype(BF16), si.astype(BF16))


def _hy_spec_kernel(hf_ref, hb_ref, cfh_ref, cfl_ref, sfh_ref, sfl_ref, kr_ref, ki_ref,
                    smh_ref, sml_ref, dfh_ref, dfl_ref, nyq_ref):
    k = pl.program_id(1)

    @pl.when(k == 0)
    def _():
        hf = hf_ref[...]
        t = lax.broadcasted_iota(jnp.int32, hf.shape, 0)
        hb = jnp.where(t == 0, 0.0, hb_ref[...])
        sm = hf + hb
        df = hf - hb
        smh_ref[...], sml_ref[...] = _split2(sm)
        dfh_ref[...], dfl_ref[...] = _split2(df)
        nyq_ref[...] = jnp.sum(jnp.where(t % 2 == 0, sm, -sm), axis=0, keepdims=True)

    kr_ref[...] = _dot3(cfh_ref[...], cfl_ref[...], smh_ref[...], sml_ref[...])
    ki = _dot3(sfh_ref[...], sfl_ref[...], dfh_ref[...], dfl_ref[...])
    row = lax.broadcasted_iota(jnp.int32, ki.shape, 0)
    ki_ref[...] = jnp.where((row == 0) & (k == 0), nyq_ref[...], ki)


def _hy_spectra(h, tabs, width):
    seq, n_cols = h.shape
    order = n_cols // (2 * width)
    cpb = width // HY_TC
    cfh, cfl, sfh, sfl = tabs[:4]
    fspec = lambda: pl.BlockSpec((HY_TK, seq), lambda c, k: (k, 0))
    return pl.pallas_call(
        _hy_spec_kernel,
        grid=(order * cpb, seq // HY_TK),
        in_specs=[pl.BlockSpec((seq, HY_TC), lambda c, k: (0, (c // cpb) * 2 * cpb + c % cpb)),
                  pl.BlockSpec((seq, HY_TC), lambda c, k: (0, (c // cpb) * 2 * cpb + cpb + c % cpb)),
                  fspec(), fspec(), fspec(), fspec()],
        out_specs=[pl.BlockSpec((HY_TK, HY_TC), lambda c, k: (k, c)),
                   pl.BlockSpec((HY_TK, HY_TC), lambda c, k: (k, c))],
        out_shape=[jax.ShapeDtypeStruct((seq, order * width), F32)] * 2,
        scratch_shapes=[pltpu.VMEM((seq, HY_TC), BF16)] * 4 + [pltpu.VMEM((1, HY_TC), F32)],
        compiler_params=_cp(("parallel", "arbitrary")),
        name="hyena_spectra",
    )(h, h, cfh, cfl, sfh, sfl)


def _hy_conv_kernel(u_ref, g_ref, wu_ref, bu_ref, wg_ref, bg_ref, kr_ref, ki_ref, fb_ref,
                    cf_ref, sf_ref, ci_ref, si_ref,
                    o_ref, ub_ref, acc_ref, *, conv_u, seq):
    k = pl.program_id(2)
    cr = CONV_ROWS

    def u_rows(r0):
        if conv_u:
            return _dwconv3(lambda a, n: u_ref[0, a:a + n, :], seq, r0, cr, wu_ref[...], bu_ref[...])
        return u_ref[0, r0:r0 + cr, :]

    @pl.when(k == 0)
    def _():
        for r0 in range(0, seq, cr):
            ub_ref[r0:r0 + cr, :] = u_rows(r0).astype(BF16)
        acc_ref[...] = jnp.zeros_like(acc_ref)

    ub = ub_ref[...]
    a = _dot(cf_ref[...], ub)
    b = _dot(sf_ref[...], ub)
    kr = kr_ref[...]
    ki = ki_ref[...]
    first = (lax.broadcasted_iota(jnp.int32, a.shape, 0) == 0) & (k == 0)
    bki = b * ki
    yc = a * kr - jnp.where(first, 0.0, bki)
    ys = jnp.where(first, bki, a * ki + b * kr)
    acc_ref[...] += _dot(ci_ref[...], yc.astype(BF16)) + _dot(si_ref[...], ys.astype(BF16))

    @pl.when(k == pl.num_programs(2) - 1)
    def _():
        for r0 in range(0, seq, cr):
            gate = _dwconv3(lambda a_, n: g_ref[0, a_:a_ + n, :], seq, r0, cr, wg_ref[...], bg_ref[...])
            o_ref[0, r0:r0 + cr, :] = gate * (acc_ref[r0:r0 + cr, :] + u_rows(r0) * fb_ref[0])


def _hy_conv(u, u_blk, g, g_blk, short_w, short_b, wu_blk, wg_blk, kr, ki, k_blk, fbias, order, tabs, width, conv_u):
    bsz, seq, _ = g.shape
    cpb = width // HY_TC
    fspec = lambda: pl.BlockSpec((HY_TK, seq), lambda b, c, k: (k, 0))
    ispec = lambda: pl.BlockSpec((seq, HY_TK), lambda b, c, k: (0, k))
    once = dict(pipeline_mode=pl.Buffered(1))
    return pl.pallas_call(
        functools.partial(_hy_conv_kernel, conv_u=conv_u, seq=seq),
        grid=(bsz, cpb, seq // HY_TK),
        in_specs=[pl.BlockSpec((1, seq, HY_TC), lambda b, c, k: (b, 0, u_blk + c), **once),
                  pl.BlockSpec((1, seq, HY_TC), lambda b, c, k: (b, 0, g_blk + c), **once),
                  pl.BlockSpec((3, HY_TC), lambda b, c, k: (0, wu_blk + c)),
                  pl.BlockSpec((1, HY_TC), lambda b, c, k: (0, wu_blk + c)),
                  pl.BlockSpec((3, HY_TC), lambda b, c, k: (0, wg_blk + c)),
                  pl.BlockSpec((1, HY_TC), lambda b, c, k: (0, wg_blk + c)),
                  pl.BlockSpec((HY_TK, HY_TC), lambda b, c, k: (k, k_blk + c)),
                  pl.BlockSpec((HY_TK, HY_TC), lambda b, c, k: (k, k_blk + c)),
                  pl.BlockSpec((1, 1, HY_TC), lambda b, c, k: (order, 0, c)),
                  fspec(), fspec(), ispec(), ispec()],
        out_specs=pl.BlockSpec((1, seq, HY_TC), lambda b, c, k: (b, 0, c)),
        out_shape=jax.ShapeDtypeStruct((bsz, seq, width), F32),
        scratch_shapes=[pltpu.VMEM((seq, HY_TC), BF16), pltpu.VMEM((seq, HY_TC), F32)],
        compiler_params=_cp(("parallel", "parallel", "arbitrary")),
        name="hyena_conv",
    )(u, g, short_w, short_b, short_w, short_b, kr, ki, fbias, tabs[0], tabs[2], tabs[4], tabs[5])


def kernel(x, c, ctx, c_ctx, mod_w, mod_b, norm_g, ffn_wg, ffn_wu, ffn_wd, final_g,
           ev_w_in, ev_w_out, s5_a_re, s5_a_im, s5_log_dt, s5_b_re, s5_b_im, s5_c_re, s5_c_im,
           s5_d, s5_glu_w, s5_glu_b, na_rpb,
           od_w_in, od_w_out, hy_short_w, hy_short_b, hy_w_in, hy_b_in, hy_w_mid, hy_b_mid,
           hy_w_out, hy_freq, hy_fbias,
           ssd_conv_w, ssd_conv_b, ssd_dt_bias, ssd_a_log, ssd_d, ssd_norm_g):
    bsz, seq, d = x.shape
    lc = ctx.shape[1]
    depth = mod_w.shape[0]
    assert depth == 2 and bsz == V7X_SUBLANES, "layer 0 = S5 || attention with context output, layer 1 = Hyena || SSD"
    assert seq % TM_WIDE == 0 and (bsz * lc) % TM == 0 and lc % CONV_ROWS == 0

    xl = x.reshape(bsz * seq, d)
    xc = ctx.reshape(bsz * lc, d)
    tps = seq // TM
    mod_l = lambda i: i // tps
    mod_c = lambda i: bsz

    cond = jnp.concatenate([c, c_ctx[None], jnp.zeros((V7X_SUBLANES - 1, d), F32)], axis=0)
    mods = _modulation(cond, mod_w, mod_b).reshape(depth, cond.shape[0], N_MOD, d)

    wg, wu, wd = ffn_wg.astype(BF16), ffn_wu.astype(BF16), ffn_wd.astype(BF16)

    def ffn(xx, layer, half, mod_of, final=False):
        return _ffn(xx, mods[layer], mod_of, norm_g[layer, 2 * half], wg, wu, wd, layer, half, final_g, final, TM)

    xl = ffn(xl, 0, 0, mod_l)
    xc = ffn(xc, 0, 0, mod_c)
    w_in = ev_w_in[0].astype(BF16)
    s5w = s5_d.shape[1] * s5_d.shape[2]
    u_l, qkv_l = _inproj_even(xl, mods[0], mod_l, norm_g[0, 1], w_in, s5w, seq, bsz, TM)
    u_c, qkv_c = _inproj_even(xc, mods[0], mod_c, norm_g[0, 1], w_in, s5w, lc, bsz, lc)
    s5p = _s5_params(s5_a_re[0], s5_a_im[0], s5_log_dt[0], s5_b_re[0], s5_b_im[0], s5_c_re[0], s5_c_im[0])
    y_c, y_l = _s5_scan(u_c, u_l, *s5p, bsz)
    glu_w = s5_glu_w[0].astype(BF16)
    a_l = _s5_glu(u_l, y_l, s5_d[0], glu_w, s5_glu_b[0], seq, bsz, TM)
    a_c = _s5_glu(u_c, y_c, s5_d[0], glu_w, s5_glu_b[0], lc, bsz, lc)
    heads = na_rpb.shape[1]
    bias = _na_bias(na_rpb[0], seq // GRID_W)
    o_l, o_c = _na_attention(qkv_l.reshape(bsz, seq, -1), qkv_c.reshape(bsz, lc, -1), bias, heads)
    w_out = ev_w_out[0].astype(BF16)
    xl = _outproj_even(a_l, o_l.reshape(bsz * seq, -1), xl, mods[0], mod_l, w_out, TM)
    xc = _outproj_even(a_c, o_c.reshape(bsz * lc, -1), xc, mods[0], mod_c, w_out, TM)
    xl = ffn(xl, 0, 1, mod_l)
    xc = ffn(xc, 0, 1, mod_c)

    xl = ffn(xl, 1, 0, mod_l)
    xc = ffn(xc, 1, 0, mod_c)
    hyw = hy_fbias.shape[2]
    inner = ssd_norm_g.shape[1]
    n_heads = ssd_d.shape[1]
    e_n = n_heads // SSD_GROUPS
    o_z = 3 * hyw
    o_xbc = o_z + inner
    o_dt = o_xbc + inner + 2 * SSD_GROUPS * SSD_STATE
    w_od = od_w_in[0]
    w_dt = w_od[:, o_dt:].reshape(d, 2, SSD_GROUPS, e_n)
    w_dt = jnp.transpose(w_dt, (0, 2, 1, 3)).reshape(d, SSD_GROUPS, 2 * e_n)
    w_dt = jnp.pad(w_dt, ((0, 0), (0, 0), (0, V7X_LANES - 2 * e_n))).reshape(d, SSD_GROUPS * V7X_LANES).astype(BF16)

    def regroup(v):
        v = jnp.transpose(v.astype(F32).reshape(2, SSD_GROUPS, e_n), (1, 0, 2)).reshape(SSD_GROUPS, 2 * e_n)
        return jnp.pad(v, ((0, 0), (0, V7X_LANES - 2 * e_n))).reshape(1, SSD_GROUPS * V7X_LANES)

    w_main = w_od[:, :o_dt].astype(BF16)
    p_l, dt_l = _inproj_odd(xl, mods[1], lambda i: i // (seq // TM_WIDE), norm_g[1, 1], w_main, w_dt, TM_WIDE)
    p_c, dt_c = _inproj_odd(xc, mods[1], mod_c, norm_g[1, 1], w_main[:, o_xbc:], w_dt, TM)
    p_l3 = p_l.reshape(bsz, seq, -1)
    y_ssd = _ssd(p_l3, dt_l.reshape(bsz, seq, -1), p_c.reshape(bsz, lc, -1), dt_c.reshape(bsz, lc, -1),
                 o_xbc, 0, ssd_conv_w[0], ssd_conv_b[0][None], regroup(ssd_dt_bias[0]),
                 regroup(-jnp.exp(ssd_a_log[0].astype(F32))),
                 jnp.repeat(ssd_d[0].astype(F32), SSD_HEAD_DIM)[None], inner)

    filt = _hy_filters(seq, hy_w_in[0], hy_b_in[0], hy_w_mid[0], hy_b_mid[0], hy_w_out[0], hy_freq[0], hyw)
    tabs = _dft_tables(seq)
    kr, ki = _hy_spectra(filt, tabs, hyw)
    cpb = hyw // HY_TC
    sw, sb = hy_short_w[0], hy_short_b[0][None]
    fb = hy_fbias[0][:, None, :]
    z1 = _hy_conv(p_l3, 2 * cpb, p_l3, 0, sw, sb, 2 * cpb, 0, kr, ki, 0, fb, 0, tabs, hyw, True)
    y_hy = _hy_conv(z1, 0, p_l3, cpb, sw, sb, 0, cpb, kr, ki, cpb, fb, 1, tabs, hyw, False)

    xl = _outproj_odd(y_hy.reshape(bsz * seq, hyw), y_ssd.reshape(bsz * seq, inner), p_l, o_z // inner,
                      ssd_norm_g[0], xl, mods[1], mod_l, od_w_out[0].astype(BF16), TM)
    xl = ffn(xl, 1, 1, mod_l, final=True)
    return xl.reshape(bsz, seq, d)
```

```python
import functools
import math

import numpy as np
import jax
import jax.numpy as jnp
from jax import lax
from jax.experimental import pallas as pl
from jax.experimental.pallas import tpu as pltpu

F32 = jnp.float32
BF16 = jnp.bfloat16

EPS = 1e-6
GRID_W = 64
N_MOD = 9
S5_GROUP = 16
S5_STATE = 64
NA_HEAD_DIM = 128
NA_KH = 8
NA_KW = 16
HY_BANDS = 16
HY_DECAY_TARGET = 1e-2
HY_FAST_PCT = 0.3
HY_SLOW_PCT = 1.5
SSD_HEAD_DIM = 64
SSD_GROUPS = 4
SSD_STATE = 128
SSD_CHUNK = 128

V7X_LANES = 128
V7X_SUBLANES = 8
V7X_VMEM_BYTES = 64 * 1024 * 1024
VMEM_LIMIT = V7X_VMEM_BYTES - 12 * 1024 * 1024

TM = 512
TM_WIDE = 1024
ROW_BLOCK = 16
TF = 512
TN = 1024
S5_TC = 32
S5_W = 512
NA_GQ = 4
HY_TC = 512
HY_TK = 256
DFT_BLOCK = 64
NEG = -1e30


def _cp(sem, vmem=VMEM_LIMIT):
    return pltpu.CompilerParams(dimension_semantics=sem, vmem_limit_bytes=vmem)


def _sigmoid(x):
    return 1.0 / (1.0 + jnp.exp(-x))


def _silu(x):
    return x * _sigmoid(x)


def _dot(a, b):
    return jnp.dot(a, b, preferred_element_type=F32)


def _dot_nt(a, b):
    return lax.dot_general(a, b, (((1,), (1,)), ((), ())), preferred_element_type=F32)


def _split2(x):
    hi = x.astype(BF16)
    lo = (x - hi.astype(F32)).astype(BF16)
    return hi, lo


def _dot3(a_hi, a_lo, b_hi, b_lo):
    return _dot(a_hi, b_hi) + (_dot(a_hi, b_lo) + _dot(a_lo, b_hi))


def _rms(x):
    return x * lax.rsqrt(jnp.mean(x * x, axis=-1, keepdims=True) + EPS)


def _row_blocks(n_rows, body):
    def step(r, c):
        body(pl.ds(pl.multiple_of(r * ROW_BLOCK, ROW_BLOCK), ROW_BLOCK))
        return c

    lax.fori_loop(0, n_rows // ROW_BLOCK, step, 0, unroll=2)


def _rms_scale_into(row_fn, stats_ref, n_rows, width):
    def part(rows):
        sq = row_fn(rows)
        sq = sq * sq
        acc = sq[:, 0:V7X_LANES]
        for j in range(1, width // V7X_LANES):
            acc = acc + sq[:, j * V7X_LANES:(j + 1) * V7X_LANES]
        stats_ref[rows, :] = acc

    _row_blocks(n_rows, part)
    tot = jnp.sum(stats_ref[...], axis=-1, keepdims=True)
    stats_ref[...] = jnp.broadcast_to(lax.rsqrt(tot * (1.0 / width) + EPS), stats_ref.shape)


def _lanes(stat, width):
    return jnp.concatenate([stat] * (width // V7X_LANES), axis=1)


def _prenorm_into(x_ref, xn_ref, g_ref, mod_ref, sub, stats_ref, aff_ref):
    n_rows, width = x_ref.shape
    _rms_scale_into(lambda rows: x_ref[rows, :], stats_ref, n_rows, width)
    gain = g_ref[...] * (1.0 + mod_ref[0, 3 * sub + 1:3 * sub + 2, :])
    aff_ref[0] = jnp.broadcast_to(gain, (ROW_BLOCK, width))
    aff_ref[1] = jnp.broadcast_to(mod_ref[0, 3 * sub:3 * sub + 1, :], (ROW_BLOCK, width))

    def body(rows):
        y = x_ref[rows, :] * _lanes(stats_ref[rows, :], width) * aff_ref[0] + aff_ref[1]
        xn_ref[rows, :] = y.astype(BF16)

    _row_blocks(n_rows, body)


def _mod_kernel(c_ref, w_ref, b_ref, o_ref):
    cs = _silu(c_ref[...]).astype(BF16)
    o_ref[0] = _dot(cs, w_ref[0].astype(BF16)) + b_ref[0]


def _modulation(cond, mod_w, mod_b):
    depth, d, n = mod_w.shape
    rows = cond.shape[0]
    return pl.pallas_call(
        _mod_kernel,
        grid=(depth, n // TN),
        in_specs=[pl.BlockSpec((rows, d), lambda l, j: (0, 0)),
                  pl.BlockSpec((1, d, TN), lambda l, j: (l, 0, j)),
                  pl.BlockSpec((1, 1, TN), lambda l, j: (l, 0, j))],
        out_specs=pl.BlockSpec((1, rows, TN), lambda l, j: (l, 0, j)),
        out_shape=jax.ShapeDtypeStruct((depth, rows, n), F32),
        compiler_params=_cp(("arbitrary", "arbitrary")),
        name="adaln_mod",
    )(cond, mod_w, mod_b.reshape(depth, 1, n))


def _ffn_kernel(x_ref, mod_ref, g_ref, wg_ref, wu_ref, wd_ref, fg_ref, o_ref, xn_ref, acc_ref, stats_ref, aff_ref,
                *, sub, final):
    f = pl.program_id(1)
    n_rows, width = x_ref.shape

    @pl.when(f == 0)
    def _():
        _prenorm_into(x_ref, xn_ref, g_ref, mod_ref, sub, stats_ref, aff_ref)
        acc_ref[...] = jnp.zeros_like(acc_ref)

    xn = xn_ref[...]
    h = (_silu(_dot(xn, wg_ref[...])) * _dot(xn, wu_ref[...])).astype(BF16)
    acc_ref[...] += _dot(h, wd_ref[...])

    @pl.when(f == pl.num_programs(1) - 1)
    def _():
        def body(rows):
            o_ref[rows, :] = x_ref[rows, :] + 0.5 * mod_ref[0, 3 * sub + 2:3 * sub + 3, :] * acc_ref[rows, :]

        _row_blocks(n_rows, body)
        if final:
            _rms_scale_into(lambda rows: o_ref[rows, :], stats_ref, n_rows, width)

            def norm(rows):
                o_ref[rows, :] = o_ref[rows, :] * _lanes(stats_ref[rows, :], width) * fg_ref[...]

            _row_blocks(n_rows, norm)


def _ffn(x, mods, mod_of_tile, norm_g, wg, wu, wd, layer, half, final_g, final, tm):
    n_tok, d = x.shape
    ff = wg.shape[3]
    sub = 2 * half
    return pl.pallas_call(
        functools.partial(_ffn_kernel, sub=sub, final=final),
        grid=(n_tok // tm, ff // TF),
        in_specs=[pl.BlockSpec((tm, d), lambda i, f: (i, 0)),
                  pl.BlockSpec((1, N_MOD, d), lambda i, f: (mod_of_tile(i), 0, 0)),
                  pl.BlockSpec((1, d), lambda i, f: (0, 0)),
                  pl.BlockSpec((None, None, d, TF), lambda i, f: (layer, half, 0, f)),
                  pl.BlockSpec((None, None, d, TF), lambda i, f: (layer, half, 0, f)),
                  pl.BlockSpec((None, None, TF, d), lambda i, f: (layer, half, f, 0)),
                  pl.BlockSpec((1, d), lambda i, f: (0, 0))],
        out_specs=pl.BlockSpec((tm, d), lambda i, f: (i, 0)),
        out_shape=jax.ShapeDtypeStruct((n_tok, d), F32),
        scratch_shapes=[pltpu.VMEM((tm, d), BF16), pltpu.VMEM((tm, d), F32), pltpu.VMEM((tm, V7X_LANES), F32),
                        pltpu.VMEM((2, ROW_BLOCK, d), F32)],
        compiler_params=_cp(("parallel", "arbitrary")),
        name="swiglu",
    )(x, mods, norm_g.reshape(1, d), wg, wu, wd, final_g.reshape(1, d))


def _inproj_even_kernel(x_ref, mod_ref, g_ref, w_ref, u_ref, qkv_ref, xn_ref, stats_ref, aff_ref, *, s5w):
    _prenorm_into(x_ref, xn_ref, g_ref, mod_ref, 1, stats_ref, aff_ref)
    xn = xn_ref[...]
    u_ref[...] = _dot(xn, w_ref[:, :s5w])
    qkv_ref[...] = _dot(xn, w_ref[:, s5w:]).astype(BF16)


def _inproj_even(x, mods, mod_of_tile, norm_g, w, s5w, seq, nb, tm):
    n_tok, d = x.shape
    n_in = w.shape[1]
    tps = seq // tm
    return pl.pallas_call(
        functools.partial(_inproj_even_kernel, s5w=s5w),
        grid=(n_tok // tm,),
        in_specs=[pl.BlockSpec((tm, d), lambda i: (i, 0)),
                  pl.BlockSpec((1, N_MOD, d), lambda i: (mod_of_tile(i), 0, 0)),
                  pl.BlockSpec((1, d), lambda i: (0, 0)),
                  pl.BlockSpec((d, n_in), lambda i: (0, 0), pipeline_mode=pl.Buffered(1))],
        out_specs=[pl.BlockSpec((tm, s5w), lambda i: (i % tps, i // tps)),
                   pl.BlockSpec((tm, n_in - s5w), lambda i: (i, 0))],
        out_shape=[jax.ShapeDtypeStruct((seq, nb * s5w), F32),
                   jax.ShapeDtypeStruct((n_tok, n_in - s5w), BF16)],
        scratch_shapes=[pltpu.VMEM((tm, d), BF16), pltpu.VMEM((tm, V7X_LANES), F32), pltpu.VMEM((2, ROW_BLOCK, d), F32)],
        compiler_params=_cp(("parallel",)),
        name="inproj_even",
    )(x, mods, norm_g.reshape(1, d), w)


def _inproj_odd_kernel(x_ref, mod_ref, g_ref, w_ref, wdt_ref, p_ref, dt_ref, xn_ref, stats_ref, aff_ref):
    n = pl.program_id(1)

    @pl.when(n == 0)
    def _():
        _prenorm_into(x_ref, xn_ref, g_ref, mod_ref, 1, stats_ref, aff_ref)
        dt_ref[...] = _dot(xn_ref[...], wdt_ref[...])

    p_ref[...] = _dot(xn_ref[...], w_ref[...])


def _inproj_odd(x, mods, mod_of_tile, norm_g, w, col0, n_in, wdt, tm):
    n_tok, d = x.shape
    n_dt = wdt.shape[1]
    blk0 = col0 // TN
    return pl.pallas_call(
        _inproj_odd_kernel,
        grid=(n_tok // tm, n_in // TN),
        in_specs=[pl.BlockSpec((tm, d), lambda i, n: (i, 0)),
                  pl.BlockSpec((1, N_MOD, d), lambda i, n: (mod_of_tile(i), 0, 0)),
                  pl.BlockSpec((1, d), lambda i, n: (0, 0)),
                  pl.BlockSpec((d, TN), lambda i, n: (0, blk0 + n)),
                  pl.BlockSpec((d, n_dt), lambda i, n: (0, 0))],
        out_specs=[pl.BlockSpec((tm, TN), lambda i, n: (i, n)),
                   pl.BlockSpec((tm, n_dt), lambda i, n: (i, 0))],
        out_shape=[jax.ShapeDtypeStruct((n_tok, n_in), F32),
                   jax.ShapeDtypeStruct((n_tok, n_dt), F32)],
        scratch_shapes=[pltpu.VMEM((tm, d), BF16), pltpu.VMEM((tm, V7X_LANES), F32), pltpu.VMEM((2, ROW_BLOCK, d), F32)],
        compiler_params=_cp(("parallel", "arbitrary")),
        name="inproj_odd",
    )(x, mods, norm_g.reshape(1, d), w, wdt)


def _outproj_even_kernel(a_ref, b_ref, x_ref, mod_ref, wa_ref, wb_ref, o_ref):
    y = _dot(a_ref[...], wa_ref[...]) + _dot(b_ref[...], wb_ref[...])
    o_ref[...] = x_ref[...] + mod_ref[0, 5:6, :] * y


def _outproj_even(a, b, x, mods, mod_of_tile, w, tm):
    n_tok, d = x.shape
    half = a.shape[1]
    return pl.pallas_call(
        _outproj_even_kernel,
        grid=(n_tok // tm,),
        in_specs=[pl.BlockSpec((tm, half), lambda i: (i, 0)),
                  pl.BlockSpec((tm, half), lambda i: (i, 0)),
                  pl.BlockSpec((tm, d), lambda i: (i, 0)),
                  pl.BlockSpec((1, N_MOD, d), lambda i: (mod_of_tile(i), 0, 0)),
                  pl.BlockSpec((half, d), lambda i: (0, 0)),
                  pl.BlockSpec((half, d), lambda i: (1, 0))],
        out_specs=pl.BlockSpec((tm, d), lambda i: (i, 0)),
        out_shape=jax.ShapeDtypeStruct((n_tok, d), F32),
        compiler_params=_cp(("parallel",)),
        name="outproj_even",
    )(a, b, x, mods, w, w)


def _outproj_odd_kernel(hy_ref, ys_ref, z_ref, ng_ref, x_ref, mod_ref, wa_ref, wb_ref, o_ref, b_ref, y_ref, stats_ref):
    n_rows, width = ys_ref.shape

    def gated(rows):
        y_ref[rows, :] = ys_ref[rows, :] * _silu(z_ref[rows, :])

    _row_blocks(n_rows, gated)
    _rms_scale_into(lambda rows: y_ref[rows, :], stats_ref, n_rows, width)

    def body(rows):
        b_ref[rows, :] = (y_ref[rows, :] * _lanes(stats_ref[rows, :], width) * ng_ref[...]).astype(BF16)

    _row_blocks(n_rows, body)
    y = _dot(hy_ref[...].astype(BF16), wa_ref[...]) + _dot(b_ref[...], wb_ref[...])
    o_ref[...] = x_ref[...] + mod_ref[0, 5:6, :] * y


def _outproj_odd(hy, ys, p, z_block, norm_g, x, mods, mod_of_tile, w, tm):
    n_tok, d = x.shape
    half = hy.shape[1]
    return pl.pallas_call(
        _outproj_odd_kernel,
        grid=(n_tok // tm,),
        in_specs=[pl.BlockSpec((tm, half), lambda i: (i, 0)),
                  pl.BlockSpec((tm, half), lambda i: (i, 0)),
                  pl.BlockSpec((tm, half), lambda i: (i, z_block)),
                  pl.BlockSpec((1, half), lambda i: (0, 0)),
                  pl.BlockSpec((tm, d), lambda i: (i, 0)),
                  pl.BlockSpec((1, N_MOD, d), lambda i: (mod_of_tile(i), 0, 0)),
                  pl.BlockSpec((half, d), lambda i: (0, 0)),
                  pl.BlockSpec((half, d), lambda i: (1, 0))],
        out_specs=pl.BlockSpec((tm, d), lambda i: (i, 0)),
        out_shape=jax.ShapeDtypeStruct((n_tok, d), F32),
        scratch_shapes=[pltpu.VMEM((tm, half), BF16), pltpu.VMEM((tm, half), F32), pltpu.VMEM((tm, V7X_LANES), F32)],
        compiler_params=_cp(("parallel",)),
        name="outproj_odd",
    )(hy, ys, p, norm_g.reshape(1, half), x, mods, w, w)


def _s5_kernel(uc_ref, ul_ref, a_ref, bre_ref, bim_ref, cre_ref, cim_ref, yc_ref, yl_ref,
               s_ref, carry_ref, ub_ref, yb_ref, *, nc_c, tc, nb, kt_n, half):
    d = pl.program_id(0)
    j = pl.program_id(1)
    kin = bre_ref.shape[2]
    kst = bre_ref.shape[3]
    ln = ub_ref.shape[2]
    w = ub_ref.shape[0] * ln
    lpk = kin // ln

    @pl.when(j == 0)
    def _():
        carry_ref[...] = jnp.zeros_like(carry_ref)

    def gather(u_ref):
        for b in range(nb):
            for cb in range(w // ln):
                ub_ref[cb, pl.ds(b, tc, stride=nb), :] = u_ref[:, b * w + cb * ln:b * w + (cb + 1) * ln]

    def scatter(y_ref):
        for b in range(nb):
            for cb in range(w // ln):
                y_ref[0, :, b * w + cb * ln:b * w + (cb + 1) * ln] = yb_ref[cb, pl.ds(b, tc, stride=nb), :]

    def run(rev):
        for kt in range(kt_n):
            uk = jnp.concatenate([ub_ref[kt * lpk + i] for i in range(lpk)], axis=1).astype(BF16)
            s_ref[:, kt * kst:(kt + 1) * kst] = _dot(uk, bre_ref[0, kt])
            s_ref[:, half + kt * kst:half + (kt + 1) * kst] = _dot(uk, bim_ref[0, kt])
            for c0 in range(kt * kst, (kt + 1) * kst, S5_W):
                a_re = jnp.broadcast_to(a_ref[0, 0:1, c0:c0 + S5_W], (nb, S5_W))
                a_im = jnp.broadcast_to(a_ref[0, 1:2, c0:c0 + S5_W], (nb, S5_W))
                s_re = carry_ref[:, c0:c0 + S5_W]
                s_im = carry_ref[:, half + c0:half + c0 + S5_W]
                for i in range(tc):
                    row = (tc - 1 - i if rev else i) * nb
                    n_re = a_re * s_re - a_im * s_im + s_ref[row:row + nb, c0:c0 + S5_W]
                    n_im = a_re * s_im + a_im * s_re + s_ref[row:row + nb, half + c0:half + c0 + S5_W]
                    s_ref[row:row + nb, c0:c0 + S5_W] = n_re
                    s_ref[row:row + nb, half + c0:half + c0 + S5_W] = n_im
                    s_re, s_im = n_re, n_im
                carry_ref[:, c0:c0 + S5_W] = s_re
                carry_ref[:, half + c0:half + c0 + S5_W] = s_im
            p_re = s_ref[:, kt * kst:(kt + 1) * kst].astype(BF16)
            p_im = s_ref[:, half + kt * kst:half + (kt + 1) * kst].astype(BF16)
            yk = _dot(p_re, cre_ref[0, kt]) + _dot(p_im, cim_ref[0, kt])
            for i in range(lpk):
                yb_ref[kt * lpk + i] = yk[:, i * ln:(i + 1) * ln]

    @pl.when(j < nc_c)
    def _():
        gather(uc_ref)

    @pl.when(j >= nc_c)
    def _():
        gather(ul_ref)

    @pl.when(d == 0)
    def _():
        run(False)

    @pl.when(d == 1)
    def _():
        run(True)

    @pl.when(j < nc_c)
    def _():
        scatter(yc_ref)

    @pl.when(j >= nc_c)
    def _():
        scatter(yl_ref)


def _s5_scan(uc, ul, a_bar, bre, bim, cre, cim, nb):
    lc, wide = uc.shape
    seq = ul.shape[0]
    w = wide // nb
    r = S5_TC * nb
    nc_c, nc_l = lc // S5_TC, seq // S5_TC
    kt_n = bre.shape[1]
    half = a_bar.shape[2]

    def c_idx(d, j):
        jc = jnp.minimum(j, nc_c - 1)
        return jnp.where(d == 0, jc, nc_c - 1 - jc)

    def l_idx(d, j):
        jl = jnp.maximum(j - nc_c, 0)
        return jnp.where(d == 0, jl, nc_l - 1 - jl)

    wspec = lambda shp: pl.BlockSpec((1,) + shp, lambda d, j: (d,) + (0,) * len(shp))
    return pl.pallas_call(
        functools.partial(_s5_kernel, nc_c=nc_c, tc=S5_TC, nb=nb, kt_n=kt_n, half=half),
        grid=(2, nc_c + nc_l),
        in_specs=[pl.BlockSpec((S5_TC, wide), lambda d, j: (c_idx(d, j), 0)),
                  pl.BlockSpec((S5_TC, wide), lambda d, j: (l_idx(d, j), 0)),
                  wspec(a_bar.shape[1:]), wspec(bre.shape[1:]), wspec(bim.shape[1:]),
                  wspec(cre.shape[1:]), wspec(cim.shape[1:])],
        out_specs=[pl.BlockSpec((1, S5_TC, wide), lambda d, j: (d, c_idx(d, j), 0)),
                   pl.BlockSpec((1, S5_TC, wide), lambda d, j: (d, l_idx(d, j), 0))],
        out_shape=[jax.ShapeDtypeStruct((2, lc, wide), F32), jax.ShapeDtypeStruct((2, seq, wide), F32)],
        scratch_shapes=[pltpu.VMEM((r, 2 * half), F32), pltpu.VMEM((nb, 2 * half), F32),
                        pltpu.VMEM((w // V7X_LANES, r, V7X_LANES), F32),
                        pltpu.VMEM((w // V7X_LANES, r, V7X_LANES), F32)],
        compiler_params=_cp(("arbitrary", "arbitrary")),
        name="s5_scan",
    )(uc, ul, a_bar, bre, bim, cre, cim)


def _s5_params(a_re, a_im, log_dt, b_re, b_im, c_re, c_im):
    l_re, l_im = a_re.astype(F32), a_im.astype(F32)
    dt = jnp.exp(log_dt.astype(F32))[..., None]
    mag = jnp.exp(l_re * dt)
    ab_re, ab_im = mag * jnp.cos(l_im * dt), mag * jnp.sin(l_im * dt)
    inv = 1.0 / (l_re * l_re + l_im * l_im)
    q_re = ((ab_re - 1.0) * l_re + ab_im * l_im) * inv
    q_im = (ab_im * l_re - (ab_re - 1.0) * l_im) * inv
    bb_re = q_re[..., None] * b_re.astype(F32) - q_im[..., None] * b_im.astype(F32)
    bb_im = q_re[..., None] * b_im.astype(F32) + q_im[..., None] * b_re.astype(F32)
    groups, p_dim, h_dim = bb_re.shape[1:]
    gl = V7X_LANES * 2 // h_dim
    kt = groups // gl
    eye = jnp.eye(gl, dtype=F32)
    a_pack = jnp.stack([ab_re.reshape(2, groups * p_dim), ab_im.reshape(2, groups * p_dim)], axis=1)

    def pack_b(x):
        x = x.reshape(2, kt, gl, p_dim, h_dim)
        return jnp.einsum('lm,dklph->dklhmp', eye, x).reshape(2, kt, gl * h_dim, gl * p_dim).astype(BF16)

    def pack_c(x):
        x = x.reshape(2, kt, gl, h_dim, p_dim)
        return jnp.einsum('lm,dklhp->dklpmh', eye, x).reshape(2, kt, gl * p_dim, gl * h_dim).astype(BF16)

    return (a_pack, pack_b(bb_re), pack_b(bb_im),
            pack_c(c_re.astype(F32)), pack_c(-c_im.astype(F32)))


def _s5_glu_kernel(u_ref, y_ref, d_ref, w_ref, b_ref, o_ref):
    y = d_ref[...] * u_ref[...] + y_ref[0] + y_ref[1]
    g = 0.5 * y * (1.0 + jnp.tanh(math.sqrt(2.0 / math.pi) * (y + 0.044715 * (y * y * y))))
    o_ref[...] = (g * _sigmoid(_dot(g.astype(BF16), w_ref[...]) + b_ref[...])).astype(BF16)


def _s5_glu(u_tm, y_tm, d_skip, glu_w, glu_b, seq, nb, tm):
    w = glu_w.shape[0]
    tps = seq // tm
    return pl.pallas_call(
        _s5_glu_kernel,
        grid=(nb, tps),
        in_specs=[pl.BlockSpec((tm, w), lambda b, i: (i, b)),
                  pl.BlockSpec((2, tm, w), lambda b, i: (0, i, b)),
                  pl.BlockSpec((1, w), lambda b, i: (0, 0)),
                  pl.BlockSpec((w, w), lambda b, i: (0, 0)),
                  pl.BlockSpec((1, w), lambda b, i: (0, 0))],
        out_specs=pl.BlockSpec((tm, w), lambda b, i: (b * tps + i, 0)),
        out_shape=jax.ShapeDtypeStruct((nb * seq, w), BF16),
        compiler_params=_cp(("parallel", "arbitrary")),
        name="s5_glu",
    )(u_tm, y_tm, d_skip.reshape(1, w), glu_w, glu_b.reshape(1, w))


def _na_groups(rows):
    wr = NA_GQ + NA_KH - 1
    plan, sigs = [], {}
    for g in range(rows // NA_GQ):
        r0 = g * NA_GQ
        ws = min(max(r0 - NA_KH // 2, 0), rows - wr)
        sig = tuple(min(max(r - NA_KH // 2, 0), rows - NA_KH) - ws for r in range(r0, r0 + NA_GQ)) + (r0 - ws,)
        tid = sigs.setdefault(sig, len(sigs))
        plan.append((r0, ws, tid))
    reps = [next(p for p in plan if p[2] == t) for t in range(len(sigs))]
    return plan, reps, wr


def _na_bias(rpb, rows):
    _, reps, wr = _na_groups(rows)
    col = np.arange(GRID_W)
    cs = np.clip(col - NA_KW // 2, 0, GRID_W - NA_KW)
    col_ok = (col[None, :] >= cs[:, None]) & (col[None, :] < cs[:, None] + NA_KW)
    dc = np.clip(col[None, :] - col[:, None] + NA_KW - 1, 0, 2 * NA_KW - 2)
    pick = (dc[None] == np.arange(2 * NA_KW - 1)[:, None, None]).astype(np.float32)
    tcol = jnp.einsum('hdc,cqk->hdqk', rpb.astype(F32), pick, precision=lax.Precision.HIGHEST)
    tcol = jnp.where(col_ok[None, None], tcol, NEG)
    dead = jnp.full(tcol[:, 0].shape, NEG, F32)
    tabs = []
    for r0, ws, _ in reps:
        row_blocks = []
        for r in range(r0, r0 + NA_GQ):
            rs = min(max(r - NA_KH // 2, 0), rows - NA_KH)
            blocks = [tcol[:, kr - r + NA_KH - 1] if rs <= kr < rs + NA_KH else dead for kr in range(ws, ws + wr)]
            row_blocks.append(jnp.concatenate(blocks, axis=-1))
        tabs.append(jnp.concatenate(row_blocks, axis=-2))
    return jnp.stack(tabs)


def _na_kernel(q_ref, k_ref, v_ref, qc_ref, kc_ref, vc_ref, bias_ref, o_ref, oc_ref, *, plan, wr, scale):
    kc = kc_ref[0]
    vc = vc_ref[0]
    nq = NA_GQ * GRID_W
    nk = wr * GRID_W
    for r0, ws, tid in plan:
        q = q_ref[0, r0 * GRID_W:r0 * GRID_W + nq, :]
        kw = k_ref[0, ws * GRID_W:ws * GRID_W + nk, :]
        vw = v_ref[0, ws * GRID_W:ws * GRID_W + nk, :]
        s1 = _dot_nt(q, kw) * scale + bias_ref[tid, 0]
        s2 = _dot_nt(q, kc) * scale
        m = jnp.maximum(jnp.max(s1, axis=-1, keepdims=True), jnp.max(s2, axis=-1, keepdims=True))
        p1 = jnp.exp(s1 - m)
        p2 = jnp.exp(s2 - m)
        den = jnp.sum(p1, axis=-1, keepdims=True) + jnp.sum(p2, axis=-1, keepdims=True)
        o = (_dot(p1.astype(BF16), vw) + _dot(p2.astype(BF16), vc)) / den
        o_ref[0, r0 * GRID_W:r0 * GRID_W + nq, :] = o.astype(BF16)
    s = _dot_nt(qc_ref[0], kc) * scale
    p = jnp.exp(s - jnp.max(s, axis=-1, keepdims=True))
    oc = _dot(p.astype(BF16), vc) / jnp.sum(p, axis=-1, keepdims=True)
    oc_ref[0] = oc.astype(BF16)


def _na_attention(qkv_l, qkv_c, bias, heads):
    bsz, seq, _ = qkv_l.shape
    lc = qkv_c.shape[1]
    hd = NA_HEAD_DIM
    plan, _, wr = _na_groups(seq // GRID_W)
    n_types = bias.shape[0]
    spec = lambda t, off: pl.BlockSpec((1, t, hd), lambda b, h: (b, 0, off + h))
    return pl.pallas_call(
        functools.partial(_na_kernel, plan=plan, wr=wr, scale=hd ** -0.5),
        grid=(bsz, heads),
        in_specs=[spec(seq, 0), spec(seq, heads), spec(seq, 2 * heads),
                  spec(lc, 0), spec(lc, heads), spec(lc, 2 * heads),
                  pl.BlockSpec((n_types, 1) + bias.shape[2:], lambda b, h: (0, h, 0, 0))],
        out_specs=[pl.BlockSpec((1, seq, hd), lambda b, h: (b, 0, h)),
                   pl.BlockSpec((1, lc, hd), lambda b, h: (b, 0, h))],
        out_shape=[jax.ShapeDtypeStruct((bsz, seq, heads * hd), BF16),
                   jax.ShapeDtypeStruct((bsz, lc, heads * hd), BF16)],
        compiler_params=_cp(("parallel", "arbitrary")),
        name="nbr_attention",
    )(qkv_l, qkv_l, qkv_l, qkv_c, qkv_c, qkv_c, bias)


def _dwconv3(load, t_len, r0, rows, w, b):
    main = load(r0, rows)
    idx = lax.broadcasted_iota(jnp.int32, main.shape, 0)
    prev = load(r0 - 1, 1) if r0 > 0 else jnp.zeros_like(main[0:1])
    nxt = load(r0 + rows, 1) if r0 + rows < t_len else jnp.zeros_like(main[0:1])
    up = jnp.where(idx == 0, prev, pltpu.roll(main, 1, 0))
    down = jnp.where(idx == rows - 1, nxt, pltpu.roll(main, rows - 1, 0))
    return w[0:1] * up + w[1:2] * main + w[2:3] * down + b


CONV_ROWS = 256


def _ssd_kernel(xc_ref, bc_ref, cc_ref, dtc_ref, xl_ref, bl_ref, cl_ref, dtl_ref,
                wx_ref, wb_ref, wc_ref, bx_ref, bb_ref, bcb_ref, dtb_ref, a_ref, dsk_ref,
                y_ref, xs_ref, bs_ref, cs_ref, dts_ref, cum_ref, st_ref, dec_ref, xw_ref, g_ref, cumt_ref,
                *, lc, seq, e_n):
    q = SSD_CHUNK
    p = SSD_HEAD_DIM
    ncc, ncl = lc // q, seq // q
    nch = ncc + ncl

    for src, dst, w_ref, b_ref in ((0, xs_ref, wx_ref, bx_ref), (1, bs_ref, wb_ref, bb_ref), (2, cs_ref, wc_ref, bcb_ref)):
        w = w_ref[...]
        b = b_ref[...]
        for base, t_len, ref in ((0, lc, (xc_ref, bc_ref, cc_ref)[src]), (lc, seq, (xl_ref, bl_ref, cl_ref)[src])):
            cr = min(CONV_ROWS, t_len)
            for r0 in range(0, t_len, cr):
                dst[base + r0:base + r0 + cr, :] = _silu(
                    _dwconv3(lambda a, n, ref=ref: ref[0, a:a + n, :], t_len, r0, cr, w, b))
    for base, t_len, ref in ((0, lc, dtc_ref), (lc, seq, dtl_ref)):
        v = ref[0] + dtb_ref[...]
        dts_ref[base:base + t_len, :] = jnp.maximum(v, 0.0) + jnp.log(1.0 + jnp.exp(-jnp.abs(v)))

    ii = lax.broadcasted_iota(jnp.int32, (q, q), 0)
    jj = lax.broadcasted_iota(jnp.int32, (q, q), 1)
    keep = (jj <= ii, jj >= ii)
    tri = tuple(jnp.where(m, 1.0, 0.0).astype(BF16) for m in keep)
    a_row = a_ref[...]
    dsk = dsk_ref[...]
    xw = e_n * p
    fwd_lane = lax.broadcasted_iota(jnp.int32, (q, V7X_LANES), 1) < e_n
    spread = jnp.where(lax.broadcasted_iota(jnp.int32, (V7X_LANES, 2 * xw), 0)
                       == lax.broadcasted_iota(jnp.int32, (V7X_LANES, 2 * xw), 1) // p, 1.0, 0.0).astype(BF16)

    head_of_lane = lax.broadcasted_iota(jnp.int32, (q, xw), 1) // p
    head_mask = [jnp.where(head_of_lane == e, 1.0, 0.0).astype(BF16) for e in range(e_n)]

    def expand(v):
        hi, lo = _split2(v)
        return _dot(hi, spread) + _dot(lo, spread)

    def chunks(body):
        def step(c, carry):
            body(c, pl.ds(pl.multiple_of(c * q, q), q))
            return carry

        lax.fori_loop(0, nch, step, 0, unroll=3 if nch % 3 == 0 else 2)

    def decay_sums(c, rows):
        dta = dts_ref[rows, :] * a_row
        d_hi = dta.astype(BF16)
        r1 = dta - d_hi.astype(F32)
        d_mid = r1.astype(BF16)
        d_lo = (r1 - d_mid.astype(F32)).astype(BF16)
        cum = [_dot(tri[k], d_hi) + (_dot(tri[k], d_mid) + _dot(tri[k], d_lo)) for k in range(2)]
        cum_ref[rows, :] = jnp.where(fwd_lane, cum[0], cum[1])

    def weighted(c, rows):
        cum = cum_ref[rows, :]
        tot = jnp.where(fwd_lane[0:1], cum[q - 1:q, :], cum[0:1, :])
        wgt = expand(dts_ref[rows, :] * jnp.exp(tot - cum))
        x = xs_ref[rows, :]
        xw_ref[rows, :] = (jnp.concatenate([x, x], axis=1) * wgt).astype(BF16)
        dec = expand(jnp.broadcast_to(jnp.exp(tot), (V7X_SUBLANES, V7X_LANES)))[0:1]
        for k in range(2):
            dec_ref[k, c] = dec[:, k * xw:(k + 1) * xw]

    def end_states(c, rows):
        st = _dot(bs_ref[rows, :].T.astype(BF16), xw_ref[rows, :])
        for k in range(2):
            st_ref[k, c] = st[:, k * xw:(k + 1) * xw]

    chunks(decay_sums)
    chunks(weighted)
    chunks(end_states)

    for k in range(2):
        def prop(i, s, k=k):
            c = i if k == 0 else jnp.where(i < ncc, ncc - 1 - i, nch - 1 - (i - ncc))
            loc = st_ref[k, c]
            st_ref[k, c] = s
            return s * dec_ref[k, c] + loc

        lax.fori_loop(0, nch, prop, jnp.zeros(st_ref.shape[2:], F32))

    def latent_chunks(body, unroll):
        def step(i, carry):
            body(i, ncc + i, pl.ds(pl.multiple_of((ncc + i) * q, q), q))
            return carry

        lax.fori_loop(0, ncl, step, 0, unroll=unroll)

    def stage(i, c, rows):
        g_ref[i] = _dot_nt(cs_ref[rows, :].astype(BF16), bs_ref[rows, :].astype(BF16))
        cumt_ref[i] = cum_ref[rows, :].T
        x = xs_ref[rows, :]
        xw_ref[rows, :] = (jnp.concatenate([x, x], axis=1) * expand(dts_ref[rows, :])).astype(BF16)

    def emit(i, c, rows):
        x = xs_ref[rows, :]
        cm = cs_ref[rows, :].astype(BF16)
        gmat = g_ref[i]
        cum = cum_ref[rows, :]
        cum_t = cumt_ref[i]
        xdt = xw_ref[rows, :]
        lhs, rhs = [], []
        for e in range(e_n):
            for k in range(2):
                col = k * e_n + e
                seg = jnp.where(keep[k], cum[:, col:col + 1] - cum_t[col:col + 1, :], NEG)
                lhs.append((gmat * jnp.exp(seg)).astype(BF16))
                rhs.append(xdt[:, k * xw:(k + 1) * xw] * head_mask[e])
        y = _dot(jnp.concatenate(lhs, axis=1), jnp.concatenate(rhs, axis=0)) + dsk * x
        eoff = expand(jnp.exp(cum))
        for k in range(2):
            y = y + _dot(cm, st_ref[k, c].astype(BF16)) * eoff[:, k * xw:(k + 1) * xw]
        y_ref[0, pl.ds(pl.multiple_of(i * q, q), q), :] = y

    latent_chunks(stage, 4)
    latent_chunks(emit, 4)


def _ssd(p_l, dt_l, p_c, dt_c, x_off_l, x_off_c, conv_w, conv_b, dtb, a_neg, d_exp, inner):
    bsz, seq, _ = p_l.shape
    lc = p_c.shape[1]
    e_n = inner // SSD_HEAD_DIM // SSD_GROUPS
    xw = e_n * SSD_HEAD_DIM
    n = SSD_STATE
    t_tot = lc + seq

    def specs(t, off):
        return [pl.BlockSpec((1, t, xw), lambda b, g: (b, 0, off // xw + g)),
                pl.BlockSpec((1, t, n), lambda b, g: (b, 0, (off + inner) // n + g)),
                pl.BlockSpec((1, t, n), lambda b, g: (b, 0, (off + inner) // n + SSD_GROUPS + g)),
                pl.BlockSpec((1, t, V7X_LANES), lambda b, g: (b, 0, g))]

    def pspecs(rows):
        return [pl.BlockSpec((rows, xw), lambda b, g: (0, g)),
                pl.BlockSpec((rows, n), lambda b, g: (0, inner // n + g)),
                pl.BlockSpec((rows, n), lambda b, g: (0, inner // n + SSD_GROUPS + g))]

    lane = lambda: pl.BlockSpec((1, V7X_LANES), lambda b, g: (0, g))
    return pl.pallas_call(
        functools.partial(_ssd_kernel, lc=lc, seq=seq, e_n=e_n),
        grid=(bsz, SSD_GROUPS),
        in_specs=specs(lc, x_off_c) + specs(seq, x_off_l) + pspecs(3) + pspecs(1)
        + [lane(), lane(), pl.BlockSpec((1, xw), lambda b, g: (0, g))],
        out_specs=pl.BlockSpec((1, seq, xw), lambda b, g: (b, 0, g)),
        out_shape=jax.ShapeDtypeStruct((bsz, seq, inner), F32),
        scratch_shapes=[pltpu.VMEM((t_tot, xw), F32), pltpu.VMEM((t_tot, n), F32), pltpu.VMEM((t_tot, n), F32),
                        pltpu.VMEM((t_tot, V7X_LANES), F32), pltpu.VMEM((t_tot, V7X_LANES), F32),
                        pltpu.VMEM((2, t_tot // SSD_CHUNK, n, xw), F32), pltpu.VMEM((2, t_tot // SSD_CHUNK, 1, xw), F32),
                        pltpu.VMEM((t_tot, 2 * xw), BF16),
                        pltpu.VMEM((seq // SSD_CHUNK, SSD_CHUNK, SSD_CHUNK), F32),
                        pltpu.VMEM((seq // SSD_CHUNK, V7X_LANES, SSD_CHUNK), F32)],
        compiler_params=_cp(("parallel", "arbitrary")),
        name="ssd",
    )(p_c, p_c, p_c, dt_c, p_l, p_l, p_l, dt_l, conv_w, conv_w, conv_w,
      conv_b, conv_b, conv_b, dtb, a_neg, d_exp)


def _hy_filter_kernel(z_ref, w_in_ref, b_in_ref, w_mid_ref, b_mid_ref, w_out_ref, freq_ref, delta_ref, o_ref, h_ref):
    j = pl.program_id(1)
    hp = lax.Precision.HIGHEST

    @pl.when(j == 0)
    def _():
        freq = freq_ref[...]
        h = jnp.sin(freq * (jnp.dot(z_ref[...], w_in_ref[...], precision=hp, preferred_element_type=F32) + b_in_ref[...]))
        for i in range(w_mid_ref.shape[0]):
            h = jnp.sin(freq * (jnp.dot(h, w_mid_ref[i], precision=hp, preferred_element_type=F32) + b_mid_ref[i]))
        h_ref[...] = h

    decay = jnp.exp(-z_ref[:, 0:1] * delta_ref[...])
    o_ref[...] = jnp.dot(h_ref[...], w_out_ref[...], precision=hp, preferred_element_type=F32) * decay


def _hy_filters(seq, w_in, b_in, w_mid, b_mid, w_out, freq, width):
    t = jnp.linspace(0.0, 1.0, seq, dtype=F32)[:, None]
    w = 2.0 * math.pi * jnp.arange(seq, dtype=F32)[:, None] / seq
    f = jnp.linspace(1e-4, HY_BANDS - 1, HY_BANDS, dtype=F32)[None, :]
    z = jnp.concatenate([t, jnp.cos(f * w), -jnp.sin(f * w)], axis=-1)
    emb, hid = w_in.shape
    pad = V7X_LANES
    z = jnp.pad(z, ((0, 0), (0, pad - emb)))
    w_in_p = jnp.pad(w_in.astype(F32), ((0, pad - emb), (0, pad - hid)))
    padv = lambda v: jnp.pad(v.astype(F32), ((0, 0), (0, pad - hid)))
    w_mid_p = jnp.pad(w_mid.astype(F32), ((0, 0), (0, pad - hid), (0, pad - hid)))
    w_out_p = jnp.pad(w_out.astype(F32), ((0, pad - hid), (0, 0)))
    n_out = w_out.shape[1]
    max_decay = math.log(HY_DECAY_TARGET) / HY_FAST_PCT
    min_decay = math.log(HY_DECAY_TARGET) / HY_SLOW_PCT
    deltas = jnp.abs(jnp.linspace(min_decay, max_decay, width, dtype=F32))[None, :]
    tr = 256
    n_mid = w_mid.shape[0]
    return pl.pallas_call(
        _hy_filter_kernel,
        grid=(seq // tr, n_out // width),
        in_specs=[pl.BlockSpec((tr, pad), lambda i, j: (i, 0)),
                  pl.BlockSpec((pad, pad), lambda i, j: (0, 0)),
                  pl.BlockSpec((1, pad), lambda i, j: (0, 0)),
                  pl.BlockSpec((n_mid, pad, pad), lambda i, j: (0, 0, 0)),
                  pl.BlockSpec((n_mid, 1, pad), lambda i, j: (0, 0, 0)),
                  pl.BlockSpec((pad, width), lambda i, j: (0, j)),
                  pl.BlockSpec((1, pad), lambda i, j: (0, 0)),
                  pl.BlockSpec((1, width), lambda i, j: (0, 0))],
        out_specs=pl.BlockSpec((tr, width), lambda i, j: (i, j)),
        out_shape=jax.ShapeDtypeStruct((seq, n_out), F32),
        scratch_shapes=[pltpu.VMEM((tr, pad), F32)],
        compiler_params=_cp(("parallel", "arbitrary")),
        name="hyena_filter",
    )(z, w_in_p, padv(b_in[None]), w_mid_p, padv(b_mid)[:, None, :], w_out_p, padv(freq[None]), deltas)


def _dft_tables(seq):
    n = 2 * seq
    k = jnp.arange(seq, dtype=jnp.int32)
    blk = DFT_BLOCK
    th = 2.0 * math.pi / n
    ang_a = th * ((blk * k[:seq // blk, None] * k[None, :]) % n).astype(F32)
    ang_b = th * ((k[:blk, None] * k[None, :]) % n).astype(F32)
    ca, sa = jnp.cos(ang_a)[:, None, :], jnp.sin(ang_a)[:, None, :]
    cb, sb = jnp.cos(ang_b)[None], jnp.sin(ang_b)[None]
    cos = (ca * cb - sa * sb).reshape(seq, seq)
    sin = (sa * cb + ca * sb).reshape(seq, seq)
    alt = jnp.where(k % 2 == 0, 1.0, -1.0).astype(F32)
    sf = jnp.where(k[:, None] == 0, alt[None, :], sin)
    wk = jnp.where(k == 0, 1.0, 2.0).astype(F32) / n
    ci = cos * wk[None, :]
    si = jnp.where(k[None, :] == 0, alt[:, None] / n, sin * (2.0 / n))
    return _split2(cos) + _split2(sf) + (ci.astype(BF16), si.astype(BF16))


def _hy_spec_kernel(hf_ref, hb_ref, cfh_ref, cfl_ref, sfh_ref, sfl_ref, kr_ref, ki_ref,
                    smh_ref, sml_ref, dfh_ref, dfl_ref, nyq_ref):
    k = pl.program_id(1)

    @pl.when(k == 0)
    def _():
        hf = hf_ref[...]
        t = lax.broadcasted_iota(jnp.int32, hf.shape, 0)
        hb = jnp.where(t == 0, 0.0, hb_ref[...])
        sm = hf + hb
        df = hf - hb
        smh_ref[...], sml_ref[...] = _split2(sm)
        dfh_ref[...], dfl_ref[...] = _split2(df)
        nyq_ref[...] = jnp.sum(jnp.where(t % 2 == 0, sm, -sm), axis=0, keepdims=True)

    kr_ref[...] = _dot3(cfh_ref[...], cfl_ref[...], smh_ref[...], sml_ref[...])
    ki = _dot3(sfh_ref[...], sfl_ref[...], dfh_ref[...], dfl_ref[...])
    row = lax.broadcasted_iota(jnp.int32, ki.shape, 0)
    ki_ref[...] = jnp.where((row == 0) & (k == 0), nyq_ref[...], ki)


def _hy_spectra(h, tabs, width):
    seq, n_cols = h.shape
    order = n_cols // (2 * width)
    cpb = width // HY_TC
    cfh, cfl, sfh, sfl = tabs[:4]
    fspec = lambda: pl.BlockSpec((HY_TK, seq), lambda c, k: (k, 0))
    return pl.pallas_call(
        _hy_spec_kernel,
        grid=(order * cpb, seq // HY_TK),
        in_specs=[pl.BlockSpec((seq, HY_TC), lambda c, k: (0, (c // cpb) * 2 * cpb + c % cpb)),
                  pl.BlockSpec((seq, HY_TC), lambda c, k: (0, (c // cpb) * 2 * cpb + cpb + c % cpb)),
                  fspec(), fspec(), fspec(), fspec()],
        out_specs=[pl.BlockSpec((HY_TK, HY_TC), lambda c, k: (k, c)),
                   pl.BlockSpec((HY_TK, HY_TC), lambda c, k: (k, c))],
        out_shape=[jax.ShapeDtypeStruct((seq, order * width), F32)] * 2,
        scratch_shapes=[pltpu.VMEM((seq, HY_TC), BF16)] * 4 + [pltpu.VMEM((1, HY_TC), F32)],
        compiler_params=_cp(("parallel", "arbitrary")),
        name="hyena_spectra",
    )(h, h, cfh, cfl, sfh, sfl)


def _hy_conv_kernel(u_ref, g_ref, wu_ref, bu_ref, wg_ref, bg_ref, kr_ref, ki_ref, fb_ref,
                    cf_ref, sf_ref, ci_ref, si_ref,
                    o_ref, ub_ref, acc_ref, *, conv_u, seq):
    k = pl.program_id(2)
    cr = CONV_ROWS

    def u_rows(r0):
        if conv_u:
            return _dwconv3(lambda a, n: u_ref[0, a:a + n, :], seq, r0, cr, wu_ref[...], bu_ref[...])
        return u_ref[0, r0:r0 + cr, :]

    @pl.when(k == 0)
    def _():
        for r0 in range(0, seq, cr):
            ub_ref[r0:r0 + cr, :] = u_rows(r0).astype(BF16)
        acc_ref[...] = jnp.zeros_like(acc_ref)

    ub = ub_ref[...]
    a = _dot(cf_ref[...], ub)
    b = _dot(sf_ref[...], ub)
    kr = kr_ref[...]
    ki = ki_ref[...]
    first = (lax.broadcasted_iota(jnp.int32, a.shape, 0) == 0) & (k == 0)
    bki = b * ki
    yc = a * kr - jnp.where(first, 0.0, bki)
    ys = jnp.where(first, bki, a * ki + b * kr)
    acc_ref[...] += _dot(ci_ref[...], yc.astype(BF16)) + _dot(si_ref[...], ys.astype(BF16))

    @pl.when(k == pl.num_programs(2) - 1)
    def _():
        for r0 in range(0, seq, cr):
            gate = _dwconv3(lambda a_, n: g_ref[0, a_:a_ + n, :], seq, r0, cr, wg_ref[...], bg_ref[...])
            o_ref[0, r0:r0 + cr, :] = gate * (acc_ref[r0:r0 + cr, :] + u_rows(r0) * fb_ref[0])


def _hy_conv(u, u_blk, g, g_blk, short_w, short_b, wu_blk, wg_blk, kr, ki, k_blk, fbias, order, tabs, width, conv_u):
    bsz, seq, _ = g.shape
    cpb = width // HY_TC
    fspec = lambda: pl.BlockSpec((HY_TK, seq), lambda b, c, k: (k, 0))
    ispec = lambda: pl.BlockSpec((seq, HY_TK), lambda b, c, k: (0, k))
    return pl.pallas_call(
        functools.partial(_hy_conv_kernel, conv_u=conv_u, seq=seq),
        grid=(bsz, cpb, seq // HY_TK),
        in_specs=[pl.BlockSpec((1, seq, HY_TC), lambda b, c, k: (b, 0, u_blk + c)),
                  pl.BlockSpec((1, seq, HY_TC), lambda b, c, k: (b, 0, g_blk + c)),
                  pl.BlockSpec((3, HY_TC), lambda b, c, k: (0, wu_blk + c)),
                  pl.BlockSpec((1, HY_TC), lambda b, c, k: (0, wu_blk + c)),
                  pl.BlockSpec((3, HY_TC), lambda b, c, k: (0, wg_blk + c)),
                  pl.BlockSpec((1, HY_TC), lambda b, c, k: (0, wg_blk + c)),
                  pl.BlockSpec((HY_TK, HY_TC), lambda b, c, k: (k, k_blk + c)),
                  pl.BlockSpec((HY_TK, HY_TC), lambda b, c, k: (k, k_blk + c)),
                  pl.BlockSpec((1, 1, HY_TC), lambda b, c, k: (order, 0, c)),
                  fspec(), fspec(), ispec(), ispec()],
        out_specs=pl.BlockSpec((1, seq, HY_TC), lambda b, c, k: (b, 0, c)),
        out_shape=jax.ShapeDtypeStruct((bsz, seq, width), F32),
        scratch_shapes=[pltpu.VMEM((seq, HY_TC), BF16), pltpu.VMEM((seq, HY_TC), F32)],
        compiler_params=_cp(("parallel", "parallel", "arbitrary")),
        name="hyena_conv",
    )(u, g, short_w, short_b, short_w, short_b, kr, ki, fbias, tabs[0], tabs[2], tabs[4], tabs[5])


def kernel(x, c, ctx, c_ctx, mod_w, mod_b, norm_g, ffn_wg, ffn_wu, ffn_wd, final_g,
           ev_w_in, ev_w_out, s5_a_re, s5_a_im, s5_log_dt, s5_b_re, s5_b_im, s5_c_re, s5_c_im,
           s5_d, s5_glu_w, s5_glu_b, na_rpb,
           od_w_in, od_w_out, hy_short_w, hy_short_b, hy_w_in, hy_b_in, hy_w_mid, hy_b_mid,
           hy_w_out, hy_freq, hy_fbias,
           ssd_conv_w, ssd_conv_b, ssd_dt_bias, ssd_a_log, ssd_d, ssd_norm_g):
    bsz, seq, d = x.shape
    lc = ctx.shape[1]
    depth = mod_w.shape[0]
    assert depth == 2 and bsz == V7X_SUBLANES, "layer 0 = S5 || attention with context output, layer 1 = Hyena || SSD"
    assert seq % TM_WIDE == 0 and (bsz * lc) % TM == 0 and lc % CONV_ROWS == 0

    xl = x.reshape(bsz * seq, d)
    xc = ctx.reshape(bsz * lc, d)
    tps = seq // TM
    mod_l = lambda i: i // tps
    mod_c = lambda i: bsz

    cond = jnp.concatenate([c, c_ctx[None], jnp.zeros((V7X_SUBLANES - 1, d), F32)], axis=0)
    mods = _modulation(cond, mod_w, mod_b).reshape(depth, cond.shape[0], N_MOD, d)

    wg, wu, wd = ffn_wg.astype(BF16), ffn_wu.astype(BF16), ffn_wd.astype(BF16)

    def ffn(xx, layer, half, mod_of, final=False):
        return _ffn(xx, mods[layer], mod_of, norm_g[layer, 2 * half], wg, wu, wd, layer, half, final_g, final, TM)

    xl = ffn(xl, 0, 0, mod_l)
    xc = ffn(xc, 0, 0, mod_c)
    w_in = ev_w_in[0].astype(BF16)
    s5w = s5_d.shape[1] * s5_d.shape[2]
    u_l, qkv_l = _inproj_even(xl, mods[0], mod_l, norm_g[0, 1], w_in, s5w, seq, bsz, TM)
    u_c, qkv_c = _inproj_even(xc, mods[0], mod_c, norm_g[0, 1], w_in, s5w, lc, bsz, lc)
    s5p = _s5_params(s5_a_re[0], s5_a_im[0], s5_log_dt[0], s5_b_re[0], s5_b_im[0], s5_c_re[0], s5_c_im[0])
    y_c, y_l = _s5_scan(u_c, u_l, *s5p, bsz)
    glu_w = s5_glu_w[0].astype(BF16)
    a_l = _s5_glu(u_l, y_l, s5_d[0], glu_w, s5_glu_b[0], seq, bsz, TM)
    a_c = _s5_glu(u_c, y_c, s5_d[0], glu_w, s5_glu_b[0], lc, bsz, lc)
    heads = na_rpb.shape[1]
    bias = _na_bias(na_rpb[0], seq // GRID_W)
    o_l, o_c = _na_attention(qkv_l.reshape(bsz, seq, -1), qkv_c.reshape(bsz, lc, -1), bias, heads)
    w_out = ev_w_out[0].astype(BF16)
    xl = _outproj_even(a_l, o_l.reshape(bsz * seq, -1), xl, mods[0], mod_l, w_out, TM)
    xc = _outproj_even(a_c, o_c.reshape(bsz * lc, -1), xc, mods[0], mod_c, w_out, TM)
    xl = ffn(xl, 0, 1, mod_l)
    xc = ffn(xc, 0, 1, mod_c)

    xl = ffn(xl, 1, 0, mod_l)
    xc = ffn(xc, 1, 0, mod_c)
    hyw = hy_fbias.shape[2]
    inner = ssd_norm_g.shape[1]
    n_heads = ssd_d.shape[1]
    e_n = n_heads // SSD_GROUPS
    o_z = 3 * hyw
    o_xbc = o_z + inner
    o_dt = o_xbc + inner + 2 * SSD_GROUPS * SSD_STATE
    w_od = od_w_in[0]
    w_dt = w_od[:, o_dt:].reshape(d, 2, SSD_GROUPS, e_n)
    w_dt = jnp.transpose(w_dt, (0, 2, 1, 3)).reshape(d, SSD_GROUPS, 2 * e_n)
    w_dt = jnp.pad(w_dt, ((0, 0), (0, 0), (0, V7X_LANES - 2 * e_n))).reshape(d, SSD_GROUPS * V7X_LANES).astype(BF16)

    def regroup(v):
        v = jnp.transpose(v.astype(F32).reshape(2, SSD_GROUPS, e_n), (1, 0, 2)).reshape(SSD_GROUPS, 2 * e_n)
        return jnp.pad(v, ((0, 0), (0, V7X_LANES - 2 * e_n))).reshape(1, SSD_GROUPS * V7X_LANES)

    w_main = w_od.astype(BF16)
    assert o_xbc % TN == 0 and o_dt % TN == 0
    p_l, dt_l = _inproj_odd(xl, mods[1], lambda i: i // (seq // TM_WIDE), norm_g[1, 1], w_main, 0, o_dt, w_dt, TM_WIDE)
    p_c, dt_c = _inproj_odd(xc, mods[1], mod_c, norm_g[1, 1], w_main, o_xbc, o_dt - o_xbc, w_dt, TM)
    p_l3 = p_l.reshape(bsz, seq, -1)
    y_ssd = _ssd(p_l3, dt_l.reshape(bsz, seq, -1), p_c.reshape(bsz, lc, -1), dt_c.reshape(bsz, lc, -1),
                 o_xbc, 0, ssd_conv_w[0], ssd_conv_b[0][None], regroup(ssd_dt_bias[0]),
                 regroup(-jnp.exp(ssd_a_log[0].astype(F32))),
                 jnp.repeat(ssd_d[0].astype(F32), SSD_HEAD_DIM)[None], inner)

    filt = _hy_filters(seq, hy_w_in[0], hy_b_in[0], hy_w_mid[0], hy_b_mid[0], hy_w_out[0], hy_freq[0], hyw)
    tabs = _dft_tables(seq)
    kr, ki = _hy_spectra(filt, tabs, hyw)
    cpb = hyw // HY_TC
    sw, sb = hy_short_w[0], hy_short_b[0][None]
    fb = hy_fbias[0][:, None, :]
    z1 = _hy_conv(p_l3, 2 * cpb, p_l3, 0, sw, sb, 2 * cpb, 0, kr, ki, 0, fb, 0, tabs, hyw, True)
    y_hy = _hy_conv(z1, 0, p_l3, cpb, sw, sb, 0, cpb, kr, ki, cpb, fb, 1, tabs, hyw, False)

    xl = _outproj_odd(y_hy.reshape(bsz * seq, hyw), y_ssd.reshape(bsz * seq, inner), p_l, o_z // inner,
                      ssd_norm_g[0], xl, mods[1], mod_l, od_w_out[0].astype(BF16), TM)
    xl = ffn(xl, 1, 1, mod_l, final=True)
    return xl.reshape(bsz, seq, d)
```

```python
import functools
import math

import numpy as np
import jax
import jax.numpy as jnp
from jax import lax
from jax.experimental import pallas as pl
from jax.experimental.pallas import tpu as pltpu

F32 = jnp.float32
BF16 = jnp.bfloat16

EPS = 1e-6
GRID_W = 64
N_MOD = 9
S5_GROUP = 16
S5_STATE = 64
NA_HEAD_DIM = 128
NA_KH = 8
NA_KW = 16
HY_BANDS = 16
HY_DECAY_TARGET = 1e-2
HY_FAST_PCT = 0.3
HY_SLOW_PCT = 1.5
SSD_HEAD_DIM = 64
SSD_GROUPS = 4
SSD_STATE = 128
SSD_CHUNK = 128

V7X_LANES = 128
V7X_SUBLANES = 8
V7X_VMEM_BYTES = 64 * 1024 * 1024
VMEM_LIMIT = V7X_VMEM_BYTES - 12 * 1024 * 1024

TM = 512
TM_WIDE = 1024
ROW_BLOCK = 16
TF = 512
TN = 1024
S5_TC = 32
S5_W = 512
NA_GQ = 4
HY_TC = 512
HY_TK = 256
DFT_BLOCK = 64
NEG = -1e30


def _cp(sem, vmem=VMEM_LIMIT):
    return pltpu.CompilerParams(dimension_semantics=sem, vmem_limit_bytes=vmem)


def _sigmoid(x):
    return 1.0 / (1.0 + jnp.exp(-x))


def _silu(x):
    return x * _sigmoid(x)


def _dot(a, b):
    return jnp.dot(a, b, preferred_element_type=F32)


def _dot_nt(a, b):
    return lax.dot_general(a, b, (((1,), (1,)), ((), ())), preferred_element_type=F32)


def _split2(x):
    hi = x.astype(BF16)
    lo = (x - hi.astype(F32)).astype(BF16)
    return hi, lo


def _dot3(a_hi, a_lo, b_hi, b_lo):
    return _dot(a_hi, b_hi) + (_dot(a_hi, b_lo) + _dot(a_lo, b_hi))


def _row_blocks(n_rows, body):
    def step(r, c):
        body(pl.ds(pl.multiple_of(r * ROW_BLOCK, ROW_BLOCK), ROW_BLOCK))
        return c

    lax.fori_loop(0, n_rows // ROW_BLOCK, step, 0, unroll=2)


def _rms_scale_into(row_fn, stats_ref, n_rows, width):
    def part(rows):
        sq = row_fn(rows)
        sq = sq * sq
        acc = sq[:, 0:V7X_LANES]
        for j in range(1, width // V7X_LANES):
            acc = acc + sq[:, j * V7X_LANES:(j + 1) * V7X_LANES]
        stats_ref[rows, :] = acc

    _row_blocks(n_rows, part)
    tot = jnp.sum(stats_ref[...], axis=-1, keepdims=True)
    stats_ref[...] = jnp.broadcast_to(lax.rsqrt(tot * (1.0 / width) + EPS), stats_ref.shape)


def _lanes(stat, width):
    return jnp.concatenate([stat] * (width // V7X_LANES), axis=1)


def _prenorm_into(x_ref, xn_ref, g_ref, mod_ref, sub, stats_ref, aff_ref):
    n_rows, width = x_ref.shape
    _rms_scale_into(lambda rows: x_ref[rows, :], stats_ref, n_rows, width)
    gain = g_ref[...] * (1.0 + mod_ref[0, 3 * sub + 1:3 * sub + 2, :])
    aff_ref[0] = jnp.broadcast_to(gain, (ROW_BLOCK, width))
    aff_ref[1] = jnp.broadcast_to(mod_ref[0, 3 * sub:3 * sub + 1, :], (ROW_BLOCK, width))

    def body(rows):
        y = x_ref[rows, :] * _lanes(stats_ref[rows, :], width) * aff_ref[0] + aff_ref[1]
        xn_ref[rows, :] = y.astype(BF16)

    _row_blocks(n_rows, body)


def _mod_kernel(c_ref, w_ref, b_ref, o_ref):
    cs = _silu(c_ref[...]).astype(BF16)
    o_ref[0] = _dot(cs, w_ref[0].astype(BF16)) + b_ref[0]


def _modulation(cond, mod_w, mod_b):
    depth, d, n = mod_w.shape
    rows = cond.shape[0]
    return pl.pallas_call(
        _mod_kernel,
        grid=(depth, n // TN),
        in_specs=[pl.BlockSpec((rows, d), lambda l, j: (0, 0)),
                  pl.BlockSpec((1, d, TN), lambda l, j: (l, 0, j)),
                  pl.BlockSpec((1, 1, TN), lambda l, j: (l, 0, j))],
        out_specs=pl.BlockSpec((1, rows, TN), lambda l, j: (l, 0, j)),
        out_shape=jax.ShapeDtypeStruct((depth, rows, n), F32),
        compiler_params=_cp(("arbitrary", "arbitrary")),
        name="adaln_mod",
    )(cond, mod_w, mod_b.reshape(depth, 1, n))


def _ffn_kernel(x_ref, mod_ref, g_ref, wg_ref, wu_ref, wd_ref, fg_ref, o_ref, xn_ref, acc_ref, stats_ref, aff_ref,
                *, sub, final):
    f = pl.program_id(1)
    n_rows, width = x_ref.shape

    @pl.when(f == 0)
    def _():
        _prenorm_into(x_ref, xn_ref, g_ref, mod_ref, sub, stats_ref, aff_ref)
        acc_ref[...] = jnp.zeros_like(acc_ref)

    xn = xn_ref[...]
    h = (_silu(_dot(xn, wg_ref[...])) * _dot(xn, wu_ref[...])).astype(BF16)
    acc_ref[...] += _dot(h, wd_ref[...])

    @pl.when(f == pl.num_programs(1) - 1)
    def _():
        def body(rows):
            o_ref[rows, :] = x_ref[rows, :] + 0.5 * mod_ref[0, 3 * sub + 2:3 * sub + 3, :] * acc_ref[rows, :]

        _row_blocks(n_rows, body)
        if final:
            _rms_scale_into(lambda rows: o_ref[rows, :], stats_ref, n_rows, width)

            def norm(rows):
                o_ref[rows, :] = o_ref[rows, :] * _lanes(stats_ref[rows, :], width) * fg_ref[...]

            _row_blocks(n_rows, norm)


def _ffn(x, mods, mod_of_tile, norm_g, wg, wu, wd, layer, half, final_g, final, tm):
    n_tok, d = x.shape
    ff = wg.shape[3]
    sub = 2 * half
    return pl.pallas_call(
        functools.partial(_ffn_kernel, sub=sub, final=final),
        grid=(n_tok // tm, ff // TF),
        in_specs=[pl.BlockSpec((tm, d), lambda i, f: (i, 0)),
                  pl.BlockSpec((1, N_MOD, d), lambda i, f: (mod_of_tile(i), 0, 0)),
                  pl.BlockSpec((1, d), lambda i, f: (0, 0)),
                  pl.BlockSpec((None, None, d, TF), lambda i, f: (layer, half, 0, f)),
                  pl.BlockSpec((None, None, d, TF), lambda i, f: (layer, half, 0, f)),
                  pl.BlockSpec((None, None, TF, d), lambda i, f: (layer, half, f, 0)),
                  pl.BlockSpec((1, d), lambda i, f: (0, 0))],
        out_specs=pl.BlockSpec((tm, d), lambda i, f: (i, 0)),
        out_shape=jax.ShapeDtypeStruct((n_tok, d), F32),
        scratch_shapes=[pltpu.VMEM((tm, d), BF16), pltpu.VMEM((tm, d), F32), pltpu.VMEM((tm, V7X_LANES), F32),
                        pltpu.VMEM((2, ROW_BLOCK, d), F32)],
        compiler_params=_cp(("parallel", "arbitrary")),
        name="swiglu",
    )(x, mods, norm_g.reshape(1, d), wg, wu, wd, final_g.reshape(1, d))


def _inproj_even_kernel(x_ref, mod_ref, g_ref, w_ref, u_ref, qkv_ref, xn_ref, stats_ref, aff_ref, *, s5w):
    _prenorm_into(x_ref, xn_ref, g_ref, mod_ref, 1, stats_ref, aff_ref)
    xn = xn_ref[...]
    u_ref[...] = _dot(xn, w_ref[:, :s5w])
    qkv_ref[...] = _dot(xn, w_ref[:, s5w:]).astype(BF16)


def _inproj_even(x, mods, mod_of_tile, norm_g, w, s5w, seq, nb, tm):
    n_tok, d = x.shape
    n_in = w.shape[1]
    tps = seq // tm
    return pl.pallas_call(
        functools.partial(_inproj_even_kernel, s5w=s5w),
        grid=(n_tok // tm,),
        in_specs=[pl.BlockSpec((tm, d), lambda i: (i, 0)),
                  pl.BlockSpec((1, N_MOD, d), lambda i: (mod_of_tile(i), 0, 0)),
                  pl.BlockSpec((1, d), lambda i: (0, 0)),
                  pl.BlockSpec((d, n_in), lambda i: (0, 0), pipeline_mode=pl.Buffered(1))],
        out_specs=[pl.BlockSpec((tm, s5w), lambda i: (i % tps, i // tps)),
                   pl.BlockSpec((tm, n_in - s5w), lambda i: (i, 0))],
        out_shape=[jax.ShapeDtypeStruct((seq, nb * s5w), F32),
                   jax.ShapeDtypeStruct((n_tok, n_in - s5w), BF16)],
        scratch_shapes=[pltpu.VMEM((tm, d), BF16), pltpu.VMEM((tm, V7X_LANES), F32), pltpu.VMEM((2, ROW_BLOCK, d), F32)],
        compiler_params=_cp(("parallel",)),
        name="inproj_even",
    )(x, mods, norm_g.reshape(1, d), w)


def _inproj_odd_kernel(x_ref, mod_ref, g_ref, w_ref, wdt_ref, p_ref, dt_ref, xn_ref, stats_ref, aff_ref):
    n = pl.program_id(1)

    @pl.when(n == 0)
    def _():
        _prenorm_into(x_ref, xn_ref, g_ref, mod_ref, 1, stats_ref, aff_ref)
        dt_ref[...] = _dot(xn_ref[...], wdt_ref[...])

    p_ref[...] = _dot(xn_ref[...], w_ref[...])


def _inproj_odd(x, mods, mod_of_tile, norm_g, w, col0, n_in, wdt, tm):
    n_tok, d = x.shape
    n_dt = wdt.shape[1]
    blk0 = col0 // TN
    return pl.pallas_call(
        _inproj_odd_kernel,
        grid=(n_tok // tm, n_in // TN),
        in_specs=[pl.BlockSpec((tm, d), lambda i, n: (i, 0)),
                  pl.BlockSpec((1, N_MOD, d), lambda i, n: (mod_of_tile(i), 0, 0)),
                  pl.BlockSpec((1, d), lambda i, n: (0, 0)),
                  pl.BlockSpec((d, TN), lambda i, n: (0, blk0 + n)),
                  pl.BlockSpec((d, n_dt), lambda i, n: (0, 0))],
        out_specs=[pl.BlockSpec((tm, TN), lambda i, n: (i, n)),
                   pl.BlockSpec((tm, n_dt), lambda i, n: (i, 0))],
        out_shape=[jax.ShapeDtypeStruct((n_tok, n_in), F32),
                   jax.ShapeDtypeStruct((n_tok, n_dt), F32)],
        scratch_shapes=[pltpu.VMEM((tm, d), BF16), pltpu.VMEM((tm, V7X_LANES), F32), pltpu.VMEM((2, ROW_BLOCK, d), F32)],
        compiler_params=_cp(("parallel", "arbitrary")),
        name="inproj_odd",
    )(x, mods, norm_g.reshape(1, d), w, wdt)


def _outproj_even_kernel(a_ref, b_ref, x_ref, mod_ref, wa_ref, wb_ref, o_ref):
    y = _dot(a_ref[...], wa_ref[...]) + _dot(b_ref[...], wb_ref[...])
    o_ref[...] = x_ref[...] + mod_ref[0, 5:6, :] * y


def _outproj_even(a, b, x, mods, mod_of_tile, w, tm):
    n_tok, d = x.shape
    half = a.shape[1]
    return pl.pallas_call(
        _outproj_even_kernel,
        grid=(n_tok // tm,),
        in_specs=[pl.BlockSpec((tm, half), lambda i: (i, 0)),
                  pl.BlockSpec((tm, half), lambda i: (i, 0)),
                  pl.BlockSpec((tm, d), lambda i: (i, 0)),
                  pl.BlockSpec((1, N_MOD, d), lambda i: (mod_of_tile(i), 0, 0)),
                  pl.BlockSpec((half, d), lambda i: (0, 0)),
                  pl.BlockSpec((half, d), lambda i: (1, 0))],
        out_specs=pl.BlockSpec((tm, d), lambda i: (i, 0)),
        out_shape=jax.ShapeDtypeStruct((n_tok, d), F32),
        compiler_params=_cp(("parallel",)),
        name="outproj_even",
    )(a, b, x, mods, w, w)


def _outproj_odd_kernel(hy_ref, ys_ref, z_ref, ng_ref, x_ref, mod_ref, wa_ref, wb_ref, o_ref, b_ref, y_ref, stats_ref):
    n_rows, width = ys_ref.shape

    def gated(rows):
        y_ref[rows, :] = ys_ref[rows, :] * _silu(z_ref[rows, :])

    _row_blocks(n_rows, gated)
    _rms_scale_into(lambda rows: y_ref[rows, :], stats_ref, n_rows, width)

    def body(rows):
        b_ref[rows, :] = (y_ref[rows, :] * _lanes(stats_ref[rows, :], width) * ng_ref[...]).astype(BF16)

    _row_blocks(n_rows, body)
    y = _dot(hy_ref[...].astype(BF16), wa_ref[...]) + _dot(b_ref[...], wb_ref[...])
    o_ref[...] = x_ref[...] + mod_ref[0, 5:6, :] * y


def _outproj_odd(hy, ys, p, z_block, norm_g, x, mods, mod_of_tile, w, tm):
    n_tok, d = x.shape
    half = hy.shape[1]
    return pl.pallas_call(
        _outproj_odd_kernel,
        grid=(n_tok // tm,),
        in_specs=[pl.BlockSpec((tm, half), lambda i: (i, 0)),
                  pl.BlockSpec((tm, half), lambda i: (i, 0)),
                  pl.BlockSpec((tm, half), lambda i: (i, z_block)),
                  pl.BlockSpec((1, half), lambda i: (0, 0)),
                  pl.BlockSpec((tm, d), lambda i: (i, 0)),
                  pl.BlockSpec((1, N_MOD, d), lambda i: (mod_of_tile(i), 0, 0)),
                  pl.BlockSpec((half, d), lambda i: (0, 0)),
                  pl.BlockSpec((half, d), lambda i: (1, 0))],
        out_specs=pl.BlockSpec((tm, d), lambda i: (i, 0)),
        out_shape=jax.ShapeDtypeStruct((n_tok, d), F32),
        scratch_shapes=[pltpu.VMEM((tm, half), BF16), pltpu.VMEM((tm, half), F32), pltpu.VMEM((tm, V7X_LANES), F32)],
        compiler_params=_cp(("parallel",)),
        name="outproj_odd",
    )(hy, ys, p, norm_g.reshape(1, half), x, mods, w, w)


def _s5_kernel(uc_ref, ul_ref, a_ref, bre_ref, bim_ref, cre_ref, cim_ref, yc_ref, yl_ref,
               s_ref, carry_ref, ub_ref, yb_ref, *, nc_c, tc, nb, kt_n, half):
    d = pl.program_id(0)
    j = pl.program_id(1)
    kin = bre_ref.shape[2]
    kst = bre_ref.shape[3]
    ln = ub_ref.shape[2]
    w = ub_ref.shape[0] * ln
    lpk = kin // ln

    @pl.when(j == 0)
    def _():
        carry_ref[...] = jnp.zeros_like(carry_ref)

    def gather(u_ref):
        for b in range(nb):
            for cb in range(w // ln):
                ub_ref[cb, pl.ds(b, tc, stride=nb), :] = u_ref[:, b * w + cb * ln:b * w + (cb + 1) * ln]

    def scatter(y_ref):
        for b in range(nb):
            for cb in range(w // ln):
                y_ref[0, :, b * w + cb * ln:b * w + (cb + 1) * ln] = yb_ref[cb, pl.ds(b, tc, stride=nb), :]

    def run(rev):
        for kt in range(kt_n):
            uk = jnp.concatenate([ub_ref[kt * lpk + i] for i in range(lpk)], axis=1).astype(BF16)
            s_ref[:, kt * kst:(kt + 1) * kst] = _dot(uk, bre_ref[0, kt])
            s_ref[:, half + kt * kst:half + (kt + 1) * kst] = _dot(uk, bim_ref[0, kt])
            for c0 in range(kt * kst, (kt + 1) * kst, S5_W):
                a_re = jnp.broadcast_to(a_ref[0, 0:1, c0:c0 + S5_W], (nb, S5_W))
                a_im = jnp.broadcast_to(a_ref[0, 1:2, c0:c0 + S5_W], (nb, S5_W))
                s_re = carry_ref[:, c0:c0 + S5_W]
                s_im = carry_ref[:, half + c0:half + c0 + S5_W]
                for i in range(tc):
                    row = (tc - 1 - i if rev else i) * nb
                    n_re = a_re * s_re - a_im * s_im + s_ref[row:row + nb, c0:c0 + S5_W]
                    n_im = a_re * s_im + a_im * s_re + s_ref[row:row + nb, half + c0:half + c0 + S5_W]
                    s_ref[row:row + nb, c0:c0 + S5_W] = n_re
                    s_ref[row:row + nb, half + c0:half + c0 + S5_W] = n_im
                    s_re, s_im = n_re, n_im
                carry_ref[:, c0:c0 + S5_W] = s_re
                carry_ref[:, half + c0:half + c0 + S5_W] = s_im
            p_re = s_ref[:, kt * kst:(kt + 1) * kst].astype(BF16)
            p_im = s_ref[:, half + kt * kst:half + (kt + 1) * kst].astype(BF16)
            yk = _dot(p_re, cre_ref[0, kt]) + _dot(p_im, cim_ref[0, kt])
            for i in range(lpk):
                yb_ref[kt * lpk + i] = yk[:, i * ln:(i + 1) * ln]

    @pl.when(j < nc_c)
    def _():
        gather(uc_ref)

    @pl.when(j >= nc_c)
    def _():
        gather(ul_ref)

    @pl.when(d == 0)
    def _():
        run(False)

    @pl.when(d == 1)
    def _():
        run(True)

    @pl.when(j < nc_c)
    def _():
        scatter(yc_ref)

    @pl.when(j >= nc_c)
    def _():
        scatter(yl_ref)


def _s5_scan(uc, ul, a_bar, bre, bim, cre, cim, nb):
    lc, wide = uc.shape
    seq = ul.shape[0]
    w = wide // nb
    r = S5_TC * nb
    nc_c, nc_l = lc // S5_TC, seq // S5_TC
    kt_n = bre.shape[1]
    half = a_bar.shape[2]

    def c_idx(d, j):
        jc = jnp.minimum(j, nc_c - 1)
        return jnp.where(d == 0, jc, nc_c - 1 - jc)

    def l_idx(d, j):
        jl = jnp.maximum(j - nc_c, 0)
        return jnp.where(d == 0, jl, nc_l - 1 - jl)

    wspec = lambda shp: pl.BlockSpec((1,) + shp, lambda d, j: (d,) + (0,) * len(shp))
    return pl.pallas_call(
        functools.partial(_s5_kernel, nc_c=nc_c, tc=S5_TC, nb=nb, kt_n=kt_n, half=half),
        grid=(2, nc_c + nc_l),
        in_specs=[pl.BlockSpec((S5_TC, wide), lambda d, j: (c_idx(d, j), 0)),
                  pl.BlockSpec((S5_TC, wide), lambda d, j: (l_idx(d, j), 0)),
                  wspec(a_bar.shape[1:]), wspec(bre.shape[1:]), wspec(bim.shape[1:]),
                  wspec(cre.shape[1:]), wspec(cim.shape[1:])],
        out_specs=[pl.BlockSpec((1, S5_TC, wide), lambda d, j: (d, c_idx(d, j), 0)),
                   pl.BlockSpec((1, S5_TC, wide), lambda d, j: (d, l_idx(d, j), 0))],
        out_shape=[jax.ShapeDtypeStruct((2, lc, wide), F32), jax.ShapeDtypeStruct((2, seq, wide), F32)],
        scratch_shapes=[pltpu.VMEM((r, 2 * half), F32), pltpu.VMEM((nb, 2 * half), F32),
                        pltpu.VMEM((w // V7X_LANES, r, V7X_LANES), F32),
                        pltpu.VMEM((w // V7X_LANES, r, V7X_LANES), F32)],
        compiler_params=_cp(("arbitrary", "arbitrary")),
        name="s5_scan",
    )(uc, ul, a_bar, bre, bim, cre, cim)


def _s5_params(a_re, a_im, log_dt, b_re, b_im, c_re, c_im):
    l_re, l_im = a_re.astype(F32), a_im.astype(F32)
    dt = jnp.exp(log_dt.astype(F32))[..., None]
    mag = jnp.exp(l_re * dt)
    ab_re, ab_im = mag * jnp.cos(l_im * dt), mag * jnp.sin(l_im * dt)
    inv = 1.0 / (l_re * l_re + l_im * l_im)
    q_re = ((ab_re - 1.0) * l_re + ab_im * l_im) * inv
    q_im = (ab_im * l_re - (ab_re - 1.0) * l_im) * inv
    bb_re = q_re[..., None] * b_re.astype(F32) - q_im[..., None] * b_im.astype(F32)
    bb_im = q_re[..., None] * b_im.astype(F32) + q_im[..., None] * b_re.astype(F32)
    groups, p_dim, h_dim = bb_re.shape[1:]
    gl = V7X_LANES * 2 // h_dim
    kt = groups // gl
    eye = jnp.eye(gl, dtype=F32)
    a_pack = jnp.stack([ab_re.reshape(2, groups * p_dim), ab_im.reshape(2, groups * p_dim)], axis=1)

    def pack_b(x):
        x = x.reshape(2, kt, gl, p_dim, h_dim)
        return jnp.einsum('lm,dklph->dklhmp', eye, x).reshape(2, kt, gl * h_dim, gl * p_dim).astype(BF16)

    def pack_c(x):
        x = x.reshape(2, kt, gl, h_dim, p_dim)
        return jnp.einsum('lm,dklhp->dklpmh', eye, x).reshape(2, kt, gl * p_dim, gl * h_dim).astype(BF16)

    return (a_pack, pack_b(bb_re), pack_b(bb_im),
            pack_c(c_re.astype(F32)), pack_c(-c_im.astype(F32)))


def _s5_glu_kernel(u_ref, y_ref, d_ref, w_ref, b_ref, o_ref):
    y = d_ref[...] * u_ref[...] + y_ref[0] + y_ref[1]
    g = 0.5 * y * (1.0 + jnp.tanh(math.sqrt(2.0 / math.pi) * (y + 0.044715 * (y * y * y))))
    o_ref[...] = (g * _sigmoid(_dot(g.astype(BF16), w_ref[...]) + b_ref[...])).astype(BF16)


def _s5_glu(u_tm, y_tm, d_skip, glu_w, glu_b, seq, nb, tm):
    w = glu_w.shape[0]
    tps = seq // tm
    return pl.pallas_call(
        _s5_glu_kernel,
        grid=(nb, tps),
        in_specs=[pl.BlockSpec((tm, w), lambda b, i: (i, b)),
                  pl.BlockSpec((2, tm, w), lambda b, i: (0, i, b)),
                  pl.BlockSpec((1, w), lambda b, i: (0, 0)),
                  pl.BlockSpec((w, w), lambda b, i: (0, 0)),
                  pl.BlockSpec((1, w), lambda b, i: (0, 0))],
        out_specs=pl.BlockSpec((tm, w), lambda b, i: (b * tps + i, 0)),
        out_shape=jax.ShapeDtypeStruct((nb * seq, w), BF16),
        compiler_params=_cp(("parallel", "arbitrary")),
        name="s5_glu",
    )(u_tm, y_tm, d_skip.reshape(1, w), glu_w, glu_b.reshape(1, w))


def _na_groups(rows):
    wr = NA_GQ + NA_KH - 1
    plan, sigs = [], {}
    for g in range(rows // NA_GQ):
        r0 = g * NA_GQ
        ws = min(max(r0 - NA_KH // 2, 0), rows - wr)
        sig = tuple(min(max(r - NA_KH // 2, 0), rows - NA_KH) - ws for r in range(r0, r0 + NA_GQ)) + (r0 - ws,)
        tid = sigs.setdefault(sig, len(sigs))
        plan.append((r0, ws, tid))
    reps = [next(p for p in plan if p[2] == t) for t in range(len(sigs))]
    return plan, reps, wr


def _na_bias(rpb, rows):
    _, reps, wr = _na_groups(rows)
    col = np.arange(GRID_W)
    cs = np.clip(col - NA_KW // 2, 0, GRID_W - NA_KW)
    col_ok = (col[None, :] >= cs[:, None]) & (col[None, :] < cs[:, None] + NA_KW)
    dc = np.clip(col[None, :] - col[:, None] + NA_KW - 1, 0, 2 * NA_KW - 2)
    pick = (dc[None] == np.arange(2 * NA_KW - 1)[:, None, None]).astype(np.float32)
    tcol = jnp.einsum('hdc,cqk->hdqk', rpb.astype(F32), pick, precision=lax.Precision.HIGHEST)
    tcol = jnp.where(col_ok[None, None], tcol, NEG)
    dead = jnp.full(tcol[:, 0].shape, NEG, F32)
    tabs = []
    for r0, ws, _ in reps:
        row_blocks = []
        for r in range(r0, r0 + NA_GQ):
            rs = min(max(r - NA_KH // 2, 0), rows - NA_KH)
            blocks = [tcol[:, kr - r + NA_KH - 1] if rs <= kr < rs + NA_KH else dead for kr in range(ws, ws + wr)]
            row_blocks.append(jnp.concatenate(blocks, axis=-1))
        tabs.append(jnp.concatenate(row_blocks, axis=-2))
    return jnp.stack(tabs)


def _na_kernel(q_ref, k_ref, v_ref, qc_ref, kc_ref, vc_ref, bias_ref, o_ref, oc_ref, *, plan, wr, scale):
    kc = kc_ref[0]
    vc = vc_ref[0]
    nq = NA_GQ * GRID_W
    nk = wr * GRID_W
    for r0, ws, tid in plan:
        q = q_ref[0, r0 * GRID_W:r0 * GRID_W + nq, :]
        kw = k_ref[0, ws * GRID_W:ws * GRID_W + nk, :]
        vw = v_ref[0, ws * GRID_W:ws * GRID_W + nk, :]
        s1 = _dot_nt(q, kw) * scale + bias_ref[tid, 0]
        s2 = _dot_nt(q, kc) * scale
        m = jnp.maximum(jnp.max(s1, axis=-1, keepdims=True), jnp.max(s2, axis=-1, keepdims=True))
        p1 = jnp.exp(s1 - m)
        p2 = jnp.exp(s2 - m)
        den = jnp.sum(p1, axis=-1, keepdims=True) + jnp.sum(p2, axis=-1, keepdims=True)
        o = (_dot(p1.astype(BF16), vw) + _dot(p2.astype(BF16), vc)) / den
        o_ref[0, r0 * GRID_W:r0 * GRID_W + nq, :] = o.astype(BF16)
    s = _dot_nt(qc_ref[0], kc) * scale
    p = jnp.exp(s - jnp.max(s, axis=-1, keepdims=True))
    oc = _dot(p.astype(BF16), vc) / jnp.sum(p, axis=-1, keepdims=True)
    oc_ref[0] = oc.astype(BF16)


def _na_attention(qkv_l, qkv_c, bias, heads):
    bsz, seq, _ = qkv_l.shape
    lc = qkv_c.shape[1]
    hd = NA_HEAD_DIM
    plan, _, wr = _na_groups(seq // GRID_W)
    n_types = bias.shape[0]
    spec = lambda t, off: pl.BlockSpec((1, t, hd), lambda b, h: (b, 0, off + h))
    return pl.pallas_call(
        functools.partial(_na_kernel, plan=plan, wr=wr, scale=hd ** -0.5),
        grid=(bsz, heads),
        in_specs=[spec(seq, 0), spec(seq, heads), spec(seq, 2 * heads),
                  spec(lc, 0), spec(lc, heads), spec(lc, 2 * heads),
                  pl.BlockSpec((n_types, 1) + bias.shape[2:], lambda b, h: (0, h, 0, 0))],
        out_specs=[pl.BlockSpec((1, seq, hd), lambda b, h: (b, 0, h)),
                   pl.BlockSpec((1, lc, hd), lambda b, h: (b, 0, h))],
        out_shape=[jax.ShapeDtypeStruct((bsz, seq, heads * hd), BF16),
                   jax.ShapeDtypeStruct((bsz, lc, heads * hd), BF16)],
        compiler_params=_cp(("parallel", "arbitrary")),
        name="nbr_attention",
    )(qkv_l, qkv_l, qkv_l, qkv_c, qkv_c, qkv_c, bias)


def _dwconv3(load, t_len, r0, rows, w, b):
    main = load(r0, rows)
    idx = lax.broadcasted_iota(jnp.int32, main.shape, 0)
    prev = load(r0 - 1, 1) if r0 > 0 else jnp.zeros_like(main[0:1])
    nxt = load(r0 + rows, 1) if r0 + rows < t_len else jnp.zeros_like(main[0:1])
    up = jnp.where(idx == 0, prev, pltpu.roll(main, 1, 0))
    down = jnp.where(idx == rows - 1, nxt, pltpu.roll(main, rows - 1, 0))
    return w[0:1] * up + w[1:2] * main + w[2:3] * down + b


CONV_ROWS = 256


def _ssd_kernel(xc_ref, bc_ref, cc_ref, dtc_ref, xl_ref, bl_ref, cl_ref, dtl_ref,
                wx_ref, wb_ref, wc_ref, bx_ref, bb_ref, bcb_ref, dtb_ref, a_ref, dsk_ref,
                y_ref, xs_ref, bs_ref, cs_ref, dts_ref, cum_ref, st_ref, dec_ref, xw_ref, g_ref, cumt_ref,
                *, lc, seq, e_n):
    q = SSD_CHUNK
    p = SSD_HEAD_DIM
    ncc, ncl = lc // q, seq // q
    nch = ncc + ncl

    for src, dst, w_ref, b_ref in ((0, xs_ref, wx_ref, bx_ref), (1, bs_ref, wb_ref, bb_ref), (2, cs_ref, wc_ref, bcb_ref)):
        w = w_ref[...]
        b = b_ref[...]
        for base, t_len, ref in ((0, lc, (xc_ref, bc_ref, cc_ref)[src]), (lc, seq, (xl_ref, bl_ref, cl_ref)[src])):
            cr = min(CONV_ROWS, t_len)
            for r0 in range(0, t_len, cr):
                dst[base + r0:base + r0 + cr, :] = _silu(
                    _dwconv3(lambda a, n, ref=ref: ref[0, a:a + n, :], t_len, r0, cr, w, b))
    lane_shift = lax.rem(V7X_LANES - 2 * e_n * pl.program_id(1), V7X_LANES)
    for base, t_len, ref in ((0, lc, dtc_ref), (lc, seq, dtl_ref)):
        v = pltpu.roll(ref[0], lane_shift, 1) + dtb_ref[...]
        dts_ref[base:base + t_len, :] = jnp.maximum(v, 0.0) + jnp.log(1.0 + jnp.exp(-jnp.abs(v)))

    ii = lax.broadcasted_iota(jnp.int32, (q, q), 0)
    jj = lax.broadcasted_iota(jnp.int32, (q, q), 1)
    keep = (jj <= ii, jj >= ii)
    tri = tuple(jnp.where(m, 1.0, 0.0).astype(BF16) for m in keep)
    a_row = a_ref[...]
    dsk = dsk_ref[...]
    xw = e_n * p
    fwd_lane = lax.broadcasted_iota(jnp.int32, (q, V7X_LANES), 1) < e_n
    spread = jnp.where(lax.broadcasted_iota(jnp.int32, (V7X_LANES, 2 * xw), 0)
                       == lax.broadcasted_iota(jnp.int32, (V7X_LANES, 2 * xw), 1) // p, 1.0, 0.0).astype(BF16)

    head_of_lane = lax.broadcasted_iota(jnp.int32, (q, xw), 1) // p
    head_mask = [jnp.where(head_of_lane == e, 1.0, 0.0).astype(BF16) for e in range(e_n)]

    def expand(v):
        hi, lo = _split2(v)
        return _dot(hi, spread) + _dot(lo, spread)

    def chunks(body):
        def step(c, carry):
            body(c, pl.ds(pl.multiple_of(c * q, q), q))
            return carry

        lax.fori_loop(0, nch, step, 0, unroll=6 if nch % 6 == 0 else 2)

    def decay_sums(c, rows):
        dta = dts_ref[rows, :] * a_row
        d_hi = dta.astype(BF16)
        r1 = dta - d_hi.astype(F32)
        d_mid = r1.astype(BF16)
        d_lo = (r1 - d_mid.astype(F32)).astype(BF16)
        cum = [_dot(tri[k], d_hi) + (_dot(tri[k], d_mid) + _dot(tri[k], d_lo)) for k in range(2)]
        cum_ref[rows, :] = jnp.where(fwd_lane, cum[0], cum[1])

    def weighted(c, rows):
        cum = cum_ref[rows, :]
        tot = jnp.where(fwd_lane[0:1], cum[q - 1:q, :], cum[0:1, :])
        wgt = expand(dts_ref[rows, :] * jnp.exp(tot - cum))
        x = xs_ref[rows, :]
        xw_ref[rows, :] = (jnp.concatenate([x, x], axis=1) * wgt).astype(BF16)
        dec = expand(jnp.broadcast_to(jnp.exp(tot), (V7X_SUBLANES, V7X_LANES)))[0:1]
        for k in range(2):
            dec_ref[k, c] = dec[:, k * xw:(k + 1) * xw]

    def end_states(c, rows):
        st = _dot(bs_ref[rows, :].T.astype(BF16), xw_ref[rows, :])
        for k in range(2):
            st_ref[k, c] = st[:, k * xw:(k + 1) * xw]

    chunks(decay_sums)
    chunks(weighted)
    chunks(end_states)

    for k in range(2):
        def prop(i, s, k=k):
            c = i if k == 0 else jnp.where(i < ncc, ncc - 1 - i, nch - 1 - (i - ncc))
            loc = st_ref[k, c]
            st_ref[k, c] = s
            return s * dec_ref[k, c] + loc

        lax.fori_loop(0, nch, prop, jnp.zeros(st_ref.shape[2:], F32))

    def latent_chunks(body, unroll):
        def step(i, carry):
            body(i, ncc + i, pl.ds(pl.multiple_of((ncc + i) * q, q), q))
            return carry

        lax.fori_loop(0, ncl, step, 0, unroll=unroll)

    def stage(i, c, rows):
        g_ref[i] = _dot_nt(cs_ref[rows, :].astype(BF16), bs_ref[rows, :].astype(BF16))
        cumt_ref[i] = cum_ref[rows, :].T
        x = xs_ref[rows, :]
        xw_ref[rows, :] = (jnp.concatenate([x, x], axis=1) * expand(dts_ref[rows, :])).astype(BF16)

    def emit(i, c, rows):
        x = xs_ref[rows, :]
        cm = cs_ref[rows, :].astype(BF16)
        gmat = g_ref[i]
        cum = cum_ref[rows, :]
        cum_t = cumt_ref[i]
        xdt = xw_ref[rows, :]
        lhs, rhs = [], []
        for e in range(e_n):
            for k in range(2):
                col = k * e_n + e
                seg = jnp.where(keep[k], cum[:, col:col + 1] - cum_t[col:col + 1, :], NEG)
                lhs.append((gmat * jnp.exp(seg)).astype(BF16))
                rhs.append(xdt[:, k * xw:(k + 1) * xw] * head_mask[e])
        y = _dot(jnp.concatenate(lhs, axis=1), jnp.concatenate(rhs, axis=0)) + dsk * x
        eoff = expand(jnp.exp(cum))
        for k in range(2):
            y = y + _dot(cm, st_ref[k, c].astype(BF16)) * eoff[:, k * xw:(k + 1) * xw]
        y_ref[0, pl.ds(pl.multiple_of(i * q, q), q), :] = y

    latent_chunks(stage, 4)
    latent_chunks(emit, 4)


def _ssd(p_l, dt_l, p_c, dt_c, x_off_l, x_off_c, conv_w, conv_b, dtb, a_neg, d_exp, inner):
    bsz, seq, _ = p_l.shape
    lc = p_c.shape[1]
    e_n = inner // SSD_HEAD_DIM // SSD_GROUPS
    xw = e_n * SSD_HEAD_DIM
    n = SSD_STATE
    t_tot = lc + seq

    def specs(t, off):
        return [pl.BlockSpec((1, t, xw), lambda b, g: (b, 0, off // xw + g)),
                pl.BlockSpec((1, t, n), lambda b, g: (b, 0, (off + inner) // n + g)),
                pl.BlockSpec((1, t, n), lambda b, g: (b, 0, (off + inner) // n + SSD_GROUPS + g)),
                pl.BlockSpec((1, t, V7X_LANES), lambda b, g: (b, 0, 0))]

    def pspecs(rows):
        return [pl.BlockSpec((rows, xw), lambda b, g: (0, g)),
                pl.BlockSpec((rows, n), lambda b, g: (0, inner // n + g)),
                pl.BlockSpec((rows, n), lambda b, g: (0, inner // n + SSD_GROUPS + g))]

    lane = lambda: pl.BlockSpec((1, V7X_LANES), lambda b, g: (0, g))
    return pl.pallas_call(
        functools.partial(_ssd_kernel, lc=lc, seq=seq, e_n=e_n),
        grid=(bsz, SSD_GROUPS),
        in_specs=specs(lc, x_off_c) + specs(seq, x_off_l) + pspecs(3) + pspecs(1)
        + [lane(), lane(), pl.BlockSpec((1, xw), lambda b, g: (0, g))],
        out_specs=pl.BlockSpec((1, seq, xw), lambda b, g: (b, 0, g)),
        out_shape=jax.ShapeDtypeStruct((bsz, seq, inner), F32),
        scratch_shapes=[pltpu.VMEM((t_tot, xw), F32), pltpu.VMEM((t_tot, n), F32), pltpu.VMEM((t_tot, n), F32),
                        pltpu.VMEM((t_tot, V7X_LANES), F32), pltpu.VMEM((t_tot, V7X_LANES), F32),
                        pltpu.VMEM((2, t_tot // SSD_CHUNK, n, xw), F32), pltpu.VMEM((2, t_tot // SSD_CHUNK, 1, xw), F32),
                        pltpu.VMEM((t_tot, 2 * xw), BF16),
                        pltpu.VMEM((seq // SSD_CHUNK, SSD_CHUNK, SSD_CHUNK), F32),
                        pltpu.VMEM((seq // SSD_CHUNK, V7X_LANES, SSD_CHUNK), F32)],
        compiler_params=_cp(("parallel", "arbitrary")),
        name="ssd",
    )(p_c, p_c, p_c, dt_c, p_l, p_l, p_l, dt_l, conv_w, conv_w, conv_w,
      conv_b, conv_b, conv_b, dtb, a_neg, d_exp)


def _hy_filter_kernel(z_ref, w_in_ref, b_in_ref, w_mid_ref, b_mid_ref, w_out_ref, freq_ref, delta_ref, o_ref, h_ref):
    j = pl.program_id(1)
    hp = lax.Precision.HIGHEST

    @pl.when(j == 0)
    def _():
        freq = freq_ref[...]
        h = jnp.sin(freq * (jnp.dot(z_ref[...], w_in_ref[...], precision=hp, preferred_element_type=F32) + b_in_ref[...]))
        for i in range(w_mid_ref.shape[0]):
            h = jnp.sin(freq * (jnp.dot(h, w_mid_ref[i], precision=hp, preferred_element_type=F32) + b_mid_ref[i]))
        h_ref[...] = h

    decay = jnp.exp(-z_ref[:, 0:1] * delta_ref[...])
    o_ref[...] = jnp.dot(h_ref[...], w_out_ref[...], precision=hp, preferred_element_type=F32) * decay


def _hy_filters(seq, w_in, b_in, w_mid, b_mid, w_out, freq, width):
    t = jnp.linspace(0.0, 1.0, seq, dtype=F32)[:, None]
    w = 2.0 * math.pi * jnp.arange(seq, dtype=F32)[:, None] / seq
    f = jnp.linspace(1e-4, HY_BANDS - 1, HY_BANDS, dtype=F32)[None, :]
    z = jnp.concatenate([t, jnp.cos(f * w), -jnp.sin(f * w)], axis=-1)
    emb, hid = w_in.shape
    pad = V7X_LANES
    z = jnp.pad(z, ((0, 0), (0, pad - emb)))
    w_in_p = jnp.pad(w_in.astype(F32), ((0, pad - emb), (0, pad - hid)))
    padv = lambda v: jnp.pad(v.astype(F32), ((0, 0), (0, pad - hid)))
    w_mid_p = jnp.pad(w_mid.astype(F32), ((0, 0), (0, pad - hid), (0, pad - hid)))
    w_out_p = jnp.pad(w_out.astype(F32), ((0, pad - hid), (0, 0)))
    n_out = w_out.shape[1]
    max_decay = math.log(HY_DECAY_TARGET) / HY_FAST_PCT
    min_decay = math.log(HY_DECAY_TARGET) / HY_SLOW_PCT
    deltas = jnp.abs(jnp.linspace(min_decay, max_decay, width, dtype=F32))[None, :]
    tr = 256
    n_mid = w_mid.shape[0]
    return pl.pallas_call(
        _hy_filter_kernel,
        grid=(seq // tr, n_out // width),
        in_specs=[pl.BlockSpec((tr, pad), lambda i, j: (i, 0)),
                  pl.BlockSpec((pad, pad), lambda i, j: (0, 0)),
                  pl.BlockSpec((1, pad), lambda i, j: (0, 0)),
                  pl.BlockSpec((n_mid, pad, pad), lambda i, j: (0, 0, 0)),
                  pl.BlockSpec((n_mid, 1, pad), lambda i, j: (0, 0, 0)),
                  pl.BlockSpec((pad, width), lambda i, j: (0, j)),
                  pl.BlockSpec((1, pad), lambda i, j: (0, 0)),
                  pl.BlockSpec((1, width), lambda i, j: (0, 0))],
        out_specs=pl.BlockSpec((tr, width), lambda i, j: (i, j)),
        out_shape=jax.ShapeDtypeStruct((seq, n_out), F32),
        scratch_shapes=[pltpu.VMEM((tr, pad), F32)],
        compiler_params=_cp(("parallel", "arbitrary")),
        name="hyena_filter",
    )(z, w_in_p, padv(b_in[None]), w_mid_p, padv(b_mid)[:, None, :], w_out_p, padv(freq[None]), deltas)


def _dft_tables(seq):
    n = 2 * seq
    k = jnp.arange(seq, dtype=jnp.int32)
    blk = DFT_BLOCK
    th = 2.0 * math.pi / n
    ang_a = th * ((blk * k[:seq // blk, None] * k[None, :]) % n).astype(F32)
    ang_b = th * ((k[:blk, None] * k[None, :]) % n).astype(F32)
    ca, sa = jnp.cos(ang_a)[:, None, :], jnp.sin(ang_a)[:, None, :]
    cb, sb = jnp.cos(ang_b)[None], jnp.sin(ang_b)[None]
    cos = (ca * cb - sa * sb).reshape(seq, seq)
    sin = (sa * cb + ca * sb).reshape(seq, seq)
    alt = jnp.where(k % 2 == 0, 1.0, -1.0).astype(F32)
    sf = jnp.where(k[:, None] == 0, alt[None, :], sin)
    wk = jnp.where(k == 0, 1.0, 2.0).astype(F32) / n
    ci = cos * wk[None, :]
    si = jnp.where(k[None, :] == 0, alt[:, None] / n, sin * (2.0 / n))
    return _split2(cos) + _split2(sf) + (ci.astype(BF16), si.astype(BF16))


def _hy_spec_kernel(hf_ref, hb_ref, cfh_ref, cfl_ref, sfh_ref, sfl_ref, kr_ref, ki_ref,
                    smh_ref, sml_ref, dfh_ref, dfl_ref, nyq_ref):
    k = pl.program_id(1)

    @pl.when(k == 0)
    def _():
        hf = hf_ref[...]
        t = lax.broadcasted_iota(jnp.int32, hf.shape, 0)
        hb = jnp.where(t == 0, 0.0, hb_ref[...])
        sm = hf + hb
        df = hf - hb
        smh_ref[...], sml_ref[...] = _split2(sm)
        dfh_ref[...], dfl_ref[...] = _split2(df)
        nyq_ref[...] = jnp.sum(jnp.where(t % 2 == 0, sm, -sm), axis=0, keepdims=True)

    kr_ref[...] = _dot3(cfh_ref[...], cfl_ref[...], smh_ref[...], sml_ref[...])
    ki = _dot3(sfh_ref[...], sfl_ref[...], dfh_ref[...], dfl_ref[...])
    row = lax.broadcasted_iota(jnp.int32, ki.shape, 0)
    ki_ref[...] = jnp.where((row == 0) & (k == 0), nyq_ref[...], ki)


def _hy_spectra(h, tabs, width):
    seq, n_cols = h.shape
    order = n_cols // (2 * width)
    cpb = width // HY_TC
    cfh, cfl, sfh, sfl = tabs[:4]
    fspec = lambda: pl.BlockSpec((HY_TK, seq), lambda c, k: (k, 0))
    return pl.pallas_call(
        _hy_spec_kernel,
        grid=(order * cpb, seq // HY_TK),
        in_specs=[pl.BlockSpec((seq, HY_TC), lambda c, k: (0, (c // cpb) * 2 * cpb + c % cpb)),
                  pl.BlockSpec((seq, HY_TC), lambda c, k: (0, (c // cpb) * 2 * cpb + cpb + c % cpb)),
                  fspec(), fspec(), fspec(), fspec()],
        out_specs=[pl.BlockSpec((HY_TK, HY_TC), lambda c, k: (k, c)),
                   pl.BlockSpec((HY_TK, HY_TC), lambda c, k: (k, c))],
        out_shape=[jax.ShapeDtypeStruct((seq, order * width), F32)] * 2,
        scratch_shapes=[pltpu.VMEM((seq, HY_TC), BF16)] * 4 + [pltpu.VMEM((1, HY_TC), F32)],
        compiler_params=_cp(("parallel", "arbitrary")),
        name="hyena_spectra",
    )(h, h, cfh, cfl, sfh, sfl)


def _hy_conv_kernel(u_ref, g_ref, wu_ref, bu_ref, wg_ref, bg_ref, kr_ref, ki_ref, fb_ref,
                    cf_ref, sf_ref, ci_ref, si_ref,
                    o_ref, ub_ref, acc_ref, *, conv_u, seq):
    k = pl.program_id(2)
    cr = CONV_ROWS

    def u_rows(r0):
        if conv_u:
            return _dwconv3(lambda a, n: u_ref[0, a:a + n, :], seq, r0, cr, wu_ref[...], bu_ref[...])
        return u_ref[0, r0:r0 + cr, :]

    @pl.when(k == 0)
    def _():
        for r0 in range(0, seq, cr):
            ub_ref[r0:r0 + cr, :] = u_rows(r0).astype(BF16)
        acc_ref[...] = jnp.zeros_like(acc_ref)

    ub = ub_ref[...]
    a = _dot(cf_ref[...], ub)
    b = _dot(sf_ref[...], ub)
    kr = kr_ref[...]
    ki = ki_ref[...]
    first = (lax.broadcasted_iota(jnp.int32, a.shape, 0) == 0) & (k == 0)
    bki = b * ki
    yc = a * kr - jnp.where(first, 0.0, bki)
    ys = jnp.where(first, bki, a * ki + b * kr)
    acc_ref[...] += _dot(ci_ref[...], yc.astype(BF16)) + _dot(si_ref[...], ys.astype(BF16))

    @pl.when(k == pl.num_programs(2) - 1)
    def _():
        for r0 in range(0, seq, cr):
            gate = _dwconv3(lambda a_, n: g_ref[0, a_:a_ + n, :], seq, r0, cr, wg_ref[...], bg_ref[...])
            o_ref[0, r0:r0 + cr, :] = gate * (acc_ref[r0:r0 + cr, :] + u_rows(r0) * fb_ref[0])


def _hy_conv(u, u_blk, g, g_blk, short_w, short_b, wu_blk, wg_blk, kr, ki, k_blk, fbias, order, tabs, width, conv_u):
    bsz, seq, _ = g.shape
    cpb = width // HY_TC
    fspec = lambda: pl.BlockSpec((HY_TK, seq), lambda b, c, k: (k, 0))
    ispec = lambda: pl.BlockSpec((seq, HY_TK), lambda b, c, k: (0, k))
    return pl.pallas_call(
        functools.partial(_hy_conv_kernel, conv_u=conv_u, seq=seq),
        grid=(bsz, cpb, seq // HY_TK),
        in_specs=[pl.BlockSpec((1, seq, HY_TC), lambda b, c, k: (b, 0, u_blk + c)),
                  pl.BlockSpec((1, seq, HY_TC), lambda b, c, k: (b, 0, g_blk + c)),
                  pl.BlockSpec((3, HY_TC), lambda b, c, k: (0, wu_blk + c)),
                  pl.BlockSpec((1, HY_TC), lambda b, c, k: (0, wu_blk + c)),
                  pl.BlockSpec((3, HY_TC), lambda b, c, k: (0, wg_blk + c)),
                  pl.BlockSpec((1, HY_TC), lambda b, c, k: (0, wg_blk + c)),
                  pl.BlockSpec((HY_TK, HY_TC), lambda b, c, k: (k, k_blk + c)),
                  pl.BlockSpec((HY_TK, HY_TC), lambda b, c, k: (k, k_blk + c)),
                  pl.BlockSpec((1, 1, HY_TC), lambda b, c, k: (order, 0, c)),
                  fspec(), fspec(), ispec(), ispec()],
        out_specs=pl.BlockSpec((1, seq, HY_TC), lambda b, c, k: (b, 0, c)),
        out_shape=jax.ShapeDtypeStruct((bsz, seq, width), F32),
        scratch_shapes=[pltpu.VMEM((seq, HY_TC), BF16), pltpu.VMEM((seq, HY_TC), F32)],
        compiler_params=_cp(("parallel", "parallel", "arbitrary")),
        name="hyena_conv",
    )(u, g, short_w, short_b, short_w, short_b, kr, ki, fbias, tabs[0], tabs[2], tabs[4], tabs[5])


def kernel(x, c, ctx, c_ctx, mod_w, mod_b, norm_g, ffn_wg, ffn_wu, ffn_wd, final_g,
           ev_w_in, ev_w_out, s5_a_re, s5_a_im, s5_log_dt, s5_b_re, s5_b_im, s5_c_re, s5_c_im,
           s5_d, s5_glu_w, s5_glu_b, na_rpb,
           od_w_in, od_w_out, hy_short_w, hy_short_b, hy_w_in, hy_b_in, hy_w_mid, hy_b_mid,
           hy_w_out, hy_freq, hy_fbias,
           ssd_conv_w, ssd_conv_b, ssd_dt_bias, ssd_a_log, ssd_d, ssd_norm_g):
    bsz, seq, d = x.shape
    lc = ctx.shape[1]
    depth = mod_w.shape[0]
    assert depth == 2 and bsz == V7X_SUBLANES, "layer 0 = S5 || attention with context output, layer 1 = Hyena || SSD"
    assert seq % TM_WIDE == 0 and (bsz * lc) % TM == 0 and lc % CONV_ROWS == 0

    xl = x.reshape(bsz * seq, d)
    xc = ctx.reshape(bsz * lc, d)
    tps = seq // TM
    mod_l = lambda i: i // tps
    mod_c = lambda i: bsz

    cond = jnp.concatenate([c, c_ctx[None], jnp.zeros((V7X_SUBLANES - 1, d), F32)], axis=0)
    mods = _modulation(cond, mod_w, mod_b).reshape(depth, cond.shape[0], N_MOD, d)

    wg, wu, wd = ffn_wg.astype(BF16), ffn_wu.astype(BF16), ffn_wd.astype(BF16)

    def ffn(xx, layer, half, mod_of, final=False):
        return _ffn(xx, mods[layer], mod_of, norm_g[layer, 2 * half], wg, wu, wd, layer, half, final_g, final, TM)

    xl = ffn(xl, 0, 0, mod_l)
    xc = ffn(xc, 0, 0, mod_c)
    w_in = ev_w_in[0].astype(BF16)
    s5w = s5_d.shape[1] * s5_d.shape[2]
    u_l, qkv_l = _inproj_even(xl, mods[0], mod_l, norm_g[0, 1], w_in, s5w, seq, bsz, TM)
    u_c, qkv_c = _inproj_even(xc, mods[0], mod_c, norm_g[0, 1], w_in, s5w, lc, bsz, lc)
    s5p = _s5_params(s5_a_re[0], s5_a_im[0], s5_log_dt[0], s5_b_re[0], s5_b_im[0], s5_c_re[0], s5_c_im[0])
    y_c, y_l = _s5_scan(u_c, u_l, *s5p, bsz)
    glu_w = s5_glu_w[0].astype(BF16)
    a_l = _s5_glu(u_l, y_l, s5_d[0], glu_w, s5_glu_b[0], seq, bsz, TM)
    a_c = _s5_glu(u_c, y_c, s5_d[0], glu_w, s5_glu_b[0], lc, bsz, lc)
    heads = na_rpb.shape[1]
    bias = _na_bias(na_rpb[0], seq // GRID_W)
    o_l, o_c = _na_attention(qkv_l.reshape(bsz, seq, -1), qkv_c.reshape(bsz, lc, -1), bias, heads)
    w_out = ev_w_out[0].astype(BF16)
    xl = _outproj_even(a_l, o_l.reshape(bsz * seq, -1), xl, mods[0], mod_l, w_out, TM)
    xc = _outproj_even(a_c, o_c.reshape(bsz * lc, -1), xc, mods[0], mod_c, w_out, TM)
    xl = ffn(xl, 0, 1, mod_l)
    xc = ffn(xc, 0, 1, mod_c)

    xl = ffn(xl, 1, 0, mod_l)
    xc = ffn(xc, 1, 0, mod_c)
    hyw = hy_fbias.shape[2]
    inner = ssd_norm_g.shape[1]
    n_heads = ssd_d.shape[1]
    e_n = n_heads // SSD_GROUPS
    o_z = 3 * hyw
    o_xbc = o_z + inner
    o_dt = o_xbc + inner + 2 * SSD_GROUPS * SSD_STATE
    w_od = od_w_in[0]
    w_dt = w_od[:, o_dt:].reshape(d, 2, SSD_GROUPS, e_n)
    w_dt = jnp.transpose(w_dt, (0, 2, 1, 3)).reshape(d, SSD_GROUPS * 2 * e_n)
    w_dt = jnp.pad(w_dt, ((0, 0), (0, V7X_LANES - SSD_GROUPS * 2 * e_n))).astype(BF16)

    def regroup(v):
        v = jnp.transpose(v.astype(F32).reshape(2, SSD_GROUPS, e_n), (1, 0, 2)).reshape(SSD_GROUPS, 2 * e_n)
        return jnp.pad(v, ((0, 0), (0, V7X_LANES - 2 * e_n))).reshape(1, SSD_GROUPS * V7X_LANES)

    w_main = w_od.astype(BF16)
    assert o_xbc % TN == 0 and o_dt % TN == 0
    p_l, dt_l = _inproj_odd(xl, mods[1], lambda i: i // (seq // TM_WIDE), norm_g[1, 1], w_main, 0, o_dt, w_dt, TM_WIDE)
    p_c, dt_c = _inproj_odd(xc, mods[1], mod_c, norm_g[1, 1], w_main, o_xbc, o_dt - o_xbc, w_dt, TM)
    p_l3 = p_l.reshape(bsz, seq, -1)
    y_ssd = _ssd(p_l3, dt_l.reshape(bsz, seq, -1), p_c.reshape(bsz, lc, -1), dt_c.reshape(bsz, lc, -1),
                 o_xbc, 0, ssd_conv_w[0], ssd_conv_b[0][None], regroup(ssd_dt_bias[0]),
                 regroup(-jnp.exp(ssd_a_log[0].astype(F32))),
                 jnp.repeat(ssd_d[0].astype(F32), SSD_HEAD_DIM)[None], inner)

    filt = _hy_filters(seq, hy_w_in[0], hy_b_in[0], hy_w_mid[0], hy_b_mid[0], hy_w_out[0], hy_freq[0], hyw)
    tabs = _dft_tables(seq)
    kr, ki = _hy_spectra(filt, tabs, hyw)
    cpb = hyw // HY_TC
    sw, sb = hy_short_w[0], hy_short_b[0][None]
    fb = hy_fbias[0][:, None, :]
    z1 = _hy_conv(p_l3, 2 * cpb, p_l3, 0, sw, sb, 2 * cpb, 0, kr, ki, 0, fb, 0, tabs, hyw, True)
    y_hy = _hy_conv(z1, 0, p_l3, cpb, sw, sb, 0, cpb, kr, ki, cpb, fb, 1, tabs, hyw, False)

    xl = _outproj_odd(y_hy.reshape(bsz * seq, hyw), y_ssd.reshape(bsz * seq, inner), p_l, o_z // inner,
                      ssd_norm_g[0], xl, mods[1], mod_l, od_w_out[0].astype(BF16), TM)
    xl = ffn(xl, 1, 1, mod_l, final=True)
    return xl.reshape(bsz, seq, d)
```
